```python
import math
import jax, jax.numpy as jnp
from jax import lax
import numpy as np

D_MODEL = 4096
BATCH = 4
SEQ = 4096
DEPTH = 2

HEAD_DIM = 128
ATTN_V_DIM = 2 * HEAD_DIM
ATTN_WIDTH = D_MODEL // 2
ATTN_HEADS = ATTN_WIDTH // ATTN_V_DIM
QK_WIDTH = ATTN_HEADS * 2 * HEAD_DIM
Q_BLOCK = 128
ROPE_THETA = 10000.0
SGU_WIDTH = D_MODEL // 4
SGU_CHUNK = 128
SGU_GROUPS = 8
SGU_GROUP_DIM = SGU_WIDTH // SGU_GROUPS
CONV_WIDTH = D_MODEL // 4
CONV_KERNEL = 31
N_BRANCHES = 3
FFN_DIM = ((8 * D_MODEL // 3 + 255) // 256) * 256
FFN_CONV_KERNEL = 3
NORM_EPS = 1e-6
LN_EPS = 1e-5
IN_SPLITS = [QK_WIDTH, 2 * QK_WIDTH, 2 * QK_WIDTH + ATTN_WIDTH,
             2 * QK_WIDTH + ATTN_WIDTH + 2 * SGU_WIDTH,
             2 * QK_WIDTH + ATTN_WIDTH + 2 * SGU_WIDTH + 2 * CONV_WIDTH]
IN_COLS = 2 * QK_WIDTH + ATTN_WIDTH + 2 * SGU_WIDTH + 2 * CONV_WIDTH + N_BRANCHES * D_MODEL

kernel_name = "hybrid_diffattn_sgu_conformer_convffn_adaln"


def rms_norm(x, g, eps=NORM_EPS):
    xf = x.astype(jnp.float32)
    y = xf * lax.rsqrt(jnp.mean(xf * xf, axis=-1, keepdims=True) + eps)
    return (y * g.astype(jnp.float32)).astype(x.dtype)


def layer_norm(x, g, b, eps=LN_EPS):
    xf = x.astype(jnp.float32)
    mu = jnp.mean(xf, axis=-1, keepdims=True)
    var = jnp.mean(jnp.square(xf - mu), axis=-1, keepdims=True)
    y = (xf - mu) * lax.rsqrt(var + eps)
    return (y * g.astype(jnp.float32) + b.astype(jnp.float32)).astype(x.dtype)


def rope(x, pos):
    half = HEAD_DIM // 2
    inv = ROPE_THETA ** (-jnp.arange(half, dtype=jnp.float32) / half)
    ang = pos.astype(jnp.float32)[:, None] * inv[None, :]
    cos, sin = jnp.cos(ang), jnp.sin(ang)
    xf = x.astype(jnp.float32)
    x1, x2 = xf[..., :half], xf[..., half:]
    out = jnp.concatenate([x1 * cos - x2 * sin, x2 * cos + x1 * sin], axis=-1)
    return out.astype(x.dtype)


def causal_depthwise_conv(x, w, b):
    k = w.shape[0]
    y = lax.conv_general_dilated(
        x, w[:, None, :].astype(x.dtype), window_strides=(1,), padding=[(k - 1, 0)],
        dimension_numbers=("NWC", "WIO", "NWC"), feature_group_count=x.shape[-1])
    return y + b.astype(x.dtype)


def diff_attention(q, k, v, lam):
    b_, h_, _, s_, d_ = q.shape
    nb = s_ // Q_BLOCK
    qb = q.reshape(b_, h_, 2, nb, Q_BLOCK, d_).transpose(3, 0, 1, 2, 4, 5)
    kpos = jnp.arange(s_)
    scale = d_ ** -0.5
    vf = v.astype(jnp.float32)

    def block(args):
        q_blk, i = args
        s = jnp.einsum("bhmqd,bhmkd->bhmqk", q_blk, k,
                       preferred_element_type=jnp.float32) * scale
        qpos = i * Q_BLOCK + jnp.arange(Q_BLOCK)
        mask = kpos[None, :] <= qpos[:, None]
        p = jax.nn.softmax(jnp.where(mask, s, -jnp.inf), axis=-1)
        a = p[:, :, 0] - lam * p[:, :, 1]
        return jnp.einsum("bhqk,bhkv->bhqv", a, vf)

    o = lax.map(block, (qb, jnp.arange(nb)))
    return o.transpose(1, 0, 3, 2, 4).reshape(b_, s_, h_, v.shape[-1])


def setup_inputs(seed: int = 0) -> dict:
    key = jax.random.key(seed)
    ks = jax.random.split(key, 32)
    D = D_MODEL

    def nrm(k, shape, scale):
        return jax.random.normal(k, shape, jnp.float32) * scale

    return {
        "x": nrm(ks[0], (BATCH, SEQ, D), 1.0),
        "c": nrm(ks[1], (BATCH, D), 1.0),
        "ada_w": nrm(ks[2], (DEPTH, D, 6 * D), 0.5 * D ** -0.5),
        "ada_b": nrm(ks[3], (DEPTH, 6 * D), 0.01),
        "norm1_g": 1.0 + nrm(ks[4], (DEPTH, D), 0.02),
        "w_in": nrm(ks[5], (DEPTH, D, IN_COLS), D ** -0.5),
        "b_gates": nrm(ks[6], (DEPTH, N_BRANCHES * D), 0.01),
        "lam_qk": nrm(ks[7], (DEPTH, 4, HEAD_DIM), 0.1),
        "attn_norm_g": 1.0 + nrm(ks[8], (DEPTH, ATTN_V_DIM), 0.02),
        "sgu_ln_g": 1.0 + nrm(ks[9], (DEPTH, SGU_WIDTH), 0.02),
        "sgu_ln_b": nrm(ks[10], (DEPTH, SGU_WIDTH), 0.01),
        "sgu_w": nrm(ks[11], (DEPTH, SGU_GROUPS, SGU_CHUNK, SGU_CHUNK), SGU_CHUNK ** -0.5),
        "sgu_b": 1.0 + nrm(ks[12], (DEPTH, SGU_GROUPS, SGU_CHUNK), 0.02),
        "conv_dw_w": nrm(ks[13], (DEPTH, CONV_KERNEL, CONV_WIDTH), CONV_KERNEL ** -0.5),
        "conv_dw_b": nrm(ks[14], (DEPTH, CONV_WIDTH), 0.01),
        "conv_ln_g": 1.0 + nrm(ks[15], (DEPTH, CONV_WIDTH), 0.02),
        "conv_ln_b": nrm(ks[16], (DEPTH, CONV_WIDTH), 0.01),
        "w_proj_attn": nrm(ks[17], (DEPTH, ATTN_WIDTH, D), ATTN_WIDTH ** -0.5),
        "w_proj_sgu": nrm(ks[18], (DEPTH, SGU_WIDTH, D), SGU_WIDTH ** -0.5),
        "w_proj_conv": nrm(ks[19], (DEPTH, CONV_WIDTH, D), CONV_WIDTH ** -0.5),
        "w_out": nrm(ks[20], (DEPTH, D, D), D ** -0.5),
        "norm2_g": 1.0 + nrm(ks[21], (DEPTH, D), 0.02),
        "ffn_up": nrm(ks[22], (DEPTH, D, 2 * FFN_DIM), D ** -0.5),
        "ffn_dw_w": nrm(ks[23], (DEPTH, FFN_CONV_KERNEL, FFN_DIM), FFN_CONV_KERNEL ** -0.5),
        "ffn_dw_b": nrm(ks[24], (DEPTH, FFN_DIM), 0.01),
        "ffn_down": nrm(ks[25], (DEPTH, FFN_DIM, D), FFN_DIM ** -0.5),
        "final_g": 1.0 + nrm(ks[26], (D,), 0.02),
    }


def reference(x, c, ada_w, ada_b, norm1_g, w_in, b_gates, lam_qk, attn_norm_g,
              sgu_ln_g, sgu_ln_b, sgu_w, sgu_b, conv_dw_w, conv_dw_b, conv_ln_g, conv_ln_b,
              w_proj_attn, w_proj_sgu, w_proj_conv, w_out, norm2_g,
              ffn_up, ffn_dw_w, ffn_dw_b, ffn_down, final_g):
    B, S, D = x.shape
    pos = jnp.arange(S)
    c_act = jax.nn.silu(c)
    tri = jnp.tril(jnp.ones((SGU_CHUNK, SGU_CHUNK), dtype=bool))

    for l in range(DEPTH):
        mod = c_act @ ada_w[l] + ada_b[l]
        sh1, sc1, g1, sh2, sc2, g2 = [m[:, None, :] for m in jnp.split(mod, 6, axis=-1)]

        h = rms_norm(x, norm1_g[l]) * (1.0 + sc1) + sh1
        z = h @ w_in[l]
        zq, zk, zv, zs, zc, zg = jnp.split(z, IN_SPLITS, axis=-1)

        q = zq.reshape(B, S, ATTN_HEADS, 2, HEAD_DIM).transpose(0, 2, 3, 1, 4)
        k = zk.reshape(B, S, ATTN_HEADS, 2, HEAD_DIM).transpose(0, 2, 3, 1, 4)
        v = zv.reshape(B, S, ATTN_HEADS, ATTN_V_DIM).transpose(0, 2, 1, 3)
        q, k = rope(q, pos), rope(k, pos)
        lam_init = 0.8 - 0.6 * math.exp(-0.3 * l)
        lq = lam_qk[l].astype(jnp.float32)
        lam = jnp.exp(jnp.sum(lq[0] * lq[1])) - jnp.exp(jnp.sum(lq[2] * lq[3])) + lam_init
        o = diff_attention(q, k, v, lam).astype(x.dtype)
        o_attn = (rms_norm(o, attn_norm_g[l]) * (1.0 - lam_init)).reshape(B, S, ATTN_WIDTH)

        zs = jax.nn.gelu(zs, approximate=False)
        u, sv = jnp.split(zs, 2, axis=-1)
        sv = layer_norm(sv, sgu_ln_g[l], sgu_ln_b[l])
        sv = sv.reshape(B, S // SGU_CHUNK, SGU_CHUNK, SGU_GROUPS, SGU_GROUP_DIM)
        ws = jnp.where(tri[None], sgu_w[l], 0.0).astype(sv.dtype)
        sp = jnp.einsum("gts,bnsgc->bntgc", ws, sv) + sgu_b[l].T[None, None, :, :, None]
        o_sgu = u * sp.reshape(B, S, SGU_WIDTH)

        ca, cb = jnp.split(zc, 2, axis=-1)
        y = ca * jax.nn.sigmoid(cb)
        y = causal_depthwise_conv(y, conv_dw_w[l], conv_dw_b[l])
        o_conv = jax.nn.silu(layer_norm(y, conv_ln_g[l], conv_ln_b[l]))

        gates = jax.nn.sigmoid(zg + b_gates[l]).reshape(B, S, N_BRANCHES, D)
        merged = (gates[:, :, 0] * (o_attn @ w_proj_attn[l])
                  + gates[:, :, 1] * (o_sgu @ w_proj_sgu[l])
                  + gates[:, :, 2] * (o_conv @ w_proj_conv[l]))
        x = x + g1 * (merged @ w_out[l])

        h2 = rms_norm(x, norm2_g[l]) * (1.0 + sc2) + sh2
        fa, fb = jnp.split(h2 @ ffn_up[l], 2, axis=-1)
        fa = causal_depthwise_conv(fa, ffn_dw_w[l], ffn_dw_b[l])
        x = x + g2 * ((jax.nn.silu(fa) * fb) @ ffn_down[l])

    return rms_norm(x, final_g)
```

```python
import functools
import math

import jax
import jax.numpy as jnp
from jax import lax
from jax.experimental import pallas as pl
from jax.experimental.pallas import tpu as pltpu

D_MODEL = 4096
HEAD_DIM = 128
ATTN_V_DIM = 2 * HEAD_DIM
ATTN_WIDTH = D_MODEL // 2
ATTN_HEADS = ATTN_WIDTH // ATTN_V_DIM
QK_WIDTH = ATTN_HEADS * 2 * HEAD_DIM
ROPE_THETA = 10000.0
SGU_WIDTH = D_MODEL // 4
SGU_CHUNK = 128
SGU_GROUPS = 8
CONV_WIDTH = D_MODEL // 4
CONV_KERNEL = 31
FFN_DIM = ((8 * D_MODEL // 3 + 255) // 256) * 256
FFN_CONV_KERNEL = 3
NORM_EPS = 1e-6
LN_EPS = 1e-5

COL_Q = 0
COL_K = QK_WIDTH
COL_V = 2 * QK_WIDTH
COL_SGU = COL_V + ATTN_WIDTH
COL_CONV = COL_SGU + 2 * SGU_WIDTH
COL_GATE = COL_CONV + 2 * CONV_WIDTH
IN_COLS = COL_GATE + 3 * D_MODEL

V7X_LANES = 128
V7X_SUBLANES = 8
V7X_VMEM_LIMIT_BYTES = 52 * 1024 * 1024
FFN_PAD = 11264
HALO = 32
NEG_BIG = -1e30

BF16 = jnp.bfloat16
F32 = jnp.float32


def _params(*sem):
    return pltpu.CompilerParams(dimension_semantics=sem, vmem_limit_bytes=V7X_VMEM_LIMIT_BYTES)


def _sigmoid(x):
    return 1.0 / (1.0 + jnp.exp(-x))


def _dot(a, b):
    return jnp.dot(a, b, preferred_element_type=F32)


def _ada_kernel(c_ref, w_ref, b_ref, o_ref):
    c = c_ref[...]
    c_act = c * _sigmoid(c)
    o_ref[...] = _dot(c_act.astype(BF16), w_ref[...].astype(BF16)) + b_ref[...]


def _ada_mod(c_pad, ada_w, ada_b):
    depth, d, n = ada_w.shape
    tn = 512
    rows = c_pad.shape[0]
    return pl.pallas_call(
        _ada_kernel,
        grid=(depth, n // tn),
        in_specs=[pl.BlockSpec((rows, d), lambda l, j: (0, 0)),
                  pl.BlockSpec((None, d, tn), lambda l, j: (l, 0, j)),
                  pl.BlockSpec((None, 1, tn), lambda l, j: (l, 0, j))],
        out_specs=pl.BlockSpec((None, rows, tn), lambda l, j: (l, 0, j)),
        out_shape=jax.ShapeDtypeStruct((depth, rows, n), F32),
        compiler_params=_params("parallel", "parallel"),
        name="ada_mod",
    )(c_pad, ada_w, ada_b.reshape(depth, 1, n))


def _norm_mod_kernel(x_ref, g_ref, mod_ref, o_ref, *, shift_row, scale_row):
    x = x_ref[...]
    y = x * lax.rsqrt(jnp.mean(x * x, axis=-1, keepdims=True) + NORM_EPS) * g_ref[...]
    sc = mod_ref[scale_row:scale_row + 1, :]
    sh = mod_ref[shift_row:shift_row + 1, :]
    o_ref[...] = (y * (1.0 + sc) + sh).astype(o_ref.dtype)


def _norm_mod(x, g, mod, seq, *, shift_row, scale_row):
    m, d = x.shape
    tm = 256
    per_batch = seq // tm
    return pl.pallas_call(
        functools.partial(_norm_mod_kernel, shift_row=shift_row, scale_row=scale_row),
        grid=(m // tm,),
        in_specs=[pl.BlockSpec((tm, d), lambda i: (i, 0)),
                  pl.BlockSpec((1, d), lambda i: (0, 0)),
                  pl.BlockSpec((None, 6, d), lambda i: (i // per_batch, 0, 0))],
        out_specs=pl.BlockSpec((tm, d), lambda i: (i, 0)),
        out_shape=jax.ShapeDtypeStruct((m, d), BF16),
        compiler_params=_params("parallel"),
        name="norm_mod",
    )(x, g.reshape(1, d), mod)


def _final_norm_kernel(x_ref, g_ref, o_ref):
    x = x_ref[...]
    o_ref[...] = x * lax.rsqrt(jnp.mean(x * x, axis=-1, keepdims=True) + NORM_EPS) * g_ref[...]


def _final_norm(x, g):
    m, d = x.shape
    tm = 256
    return pl.pallas_call(
        _final_norm_kernel,
        grid=(m // tm,),
        in_specs=[pl.BlockSpec((tm, d), lambda i: (i, 0)),
                  pl.BlockSpec((1, d), lambda i: (0, 0))],
        out_specs=pl.BlockSpec((tm, d), lambda i: (i, 0)),
        out_shape=jax.ShapeDtypeStruct((m, d), F32),
        compiler_params=_params("parallel"),
        name="final_norm",
    )(x, g.reshape(1, d))


def _mm_kernel(a_ref, b_ref, o_ref):
    o_ref[...] = _dot(a_ref[...], b_ref[...]).astype(o_ref.dtype)


def _matmul(a, b, *, tm, tn, out_dtype, name):
    m, k = a.shape
    _, n = b.shape
    return pl.pallas_call(
        _mm_kernel,
        grid=(m // tm, n // tn),
        in_specs=[pl.BlockSpec((tm, k), lambda i, j: (i, 0)),
                  pl.BlockSpec((k, tn), lambda i, j: (0, j))],
        out_specs=pl.BlockSpec((tm, tn), lambda i, j: (i, j)),
        out_shape=jax.ShapeDtypeStruct((m, n), out_dtype),
        compiler_params=_params("parallel", "parallel"),
        name=name,
    )(a, b)


def _rope_kernel(z_ref, cos_ref, sin_ref, o_ref, *, q_scale):
    cos = cos_ref[...]
    sin = sin_ref[...]
    n_groups = z_ref.shape[1] // HEAD_DIM
    for g in range(n_groups):
        cols = slice(g * HEAD_DIM, (g + 1) * HEAD_DIM)
        x = z_ref[:, cols].astype(F32)
        r = x * cos + pltpu.roll(x, HEAD_DIM // 2, axis=1) * sin
        if g < QK_WIDTH // HEAD_DIM:
            r = r * q_scale
        o_ref[:, cols] = r.astype(o_ref.dtype)


def _rope(z, cos_t, sin_t, seq):
    m = z.shape[0]
    tm = 512
    per_batch = seq // tm
    width = 2 * QK_WIDTH
    return pl.pallas_call(
        functools.partial(_rope_kernel, q_scale=HEAD_DIM ** -0.5),
        grid=(m // tm,),
        in_specs=[pl.BlockSpec((tm, width), lambda i: (i, 0)),
                  pl.BlockSpec((tm, HEAD_DIM), lambda i: (i % per_batch, 0)),
                  pl.BlockSpec((tm, HEAD_DIM), lambda i: (i % per_batch, 0))],
        out_specs=pl.BlockSpec((tm, width), lambda i: (i, 0)),
        out_shape=jax.ShapeDtypeStruct((m, width), BF16),
        compiler_params=_params("parallel"),
        name="rope",
    )(z, cos_t, sin_t)


def _attn_kernel(q_ref, k_ref, v_ref, lamqk_ref, g_ref, o_ref,
                 m_ref, l_ref, acc_ref, *, tq, tk, lam_init):
    qi = pl.program_id(2)

    m_ref[...] = jnp.full(m_ref.shape, NEG_BIG, F32)
    l_ref[...] = jnp.zeros(l_ref.shape, F32)
    acc_ref[...] = jnp.zeros(acc_ref.shape, F32)

    def step(kb, masked):
        start = pl.multiple_of(kb * tk, tk)
        v = v_ref[pl.ds(start, tk), :]
        for mp in range(2):
            cols = slice(mp * HEAD_DIM, (mp + 1) * HEAD_DIM)
            q = q_ref[:, cols]
            k = k_ref[pl.ds(start, tk), cols]
            s = lax.dot_general(q, k, (((1,), (1,)), ((), ())), preferred_element_type=F32)
            if masked:
                row = lax.broadcasted_iota(jnp.int32, (tq, tk), 0)
                col = lax.broadcasted_iota(jnp.int32, (tq, tk), 1)
                s = jnp.where(col <= row, s, NEG_BIG)
            m_prev = m_ref[mp]
            m_new = jnp.maximum(m_prev, jnp.max(s, axis=-1, keepdims=True))
            alpha = jnp.exp(m_prev - m_new)
            p = jnp.exp(s - m_new)
            l_ref[mp] = alpha * l_ref[mp] + jnp.sum(p, axis=-1, keepdims=True)
            acc_ref[mp] = alpha * acc_ref[mp] + _dot(p.astype(BF16), v)
            m_ref[mp] = m_new

    def body(kb, carry):
        step(kb, masked=False)
        return carry

    lax.fori_loop(0, qi, body, 0)
    step(qi, masked=True)

    lq = lamqk_ref[...]
    lam = (jnp.exp(jnp.sum(lq[0:1, :] * lq[1:2, :], axis=-1, keepdims=True))
           - jnp.exp(jnp.sum(lq[2:3, :] * lq[3:4, :], axis=-1, keepdims=True)) + lam_init)
    o = acc_ref[0] / l_ref[0] - lam * (acc_ref[1] / l_ref[1])
    y = o * lax.rsqrt(jnp.mean(o * o, axis=-1, keepdims=True) + NORM_EPS) * g_ref[...]
    o_ref[...] = (y * (1.0 - lam_init)).astype(o_ref.dtype)


def _attention(qk, z, lam_qk, attn_norm_g, batch, seq, lam_init):
    m = qk.shape[0]
    tq = tk = 512
    nq = seq // tq
    kcol0 = QK_WIDTH // ATTN_V_DIM
    vcol0 = COL_V // ATTN_V_DIM
    return pl.pallas_call(
        functools.partial(_attn_kernel, tq=tq, tk=tk, lam_init=lam_init),
        grid=(batch, ATTN_HEADS, nq),
        in_specs=[pl.BlockSpec((tq, ATTN_V_DIM), lambda b, h, i: (b * nq + i, h)),
                  pl.BlockSpec((seq, ATTN_V_DIM), lambda b, h, i: (b, kcol0 + h)),
                  pl.BlockSpec((seq, ATTN_V_DIM), lambda b, h, i: (b, vcol0 + h)),
                  pl.BlockSpec((4, HEAD_DIM), lambda b, h, i: (0, 0)),
                  pl.BlockSpec((1, ATTN_V_DIM), lambda b, h, i: (0, 0))],
        out_specs=pl.BlockSpec((tq, ATTN_V_DIM), lambda b, h, i: (b * nq + i, h)),
        out_shape=jax.ShapeDtypeStruct((m, ATTN_WIDTH), BF16),
        scratch_shapes=[pltpu.VMEM((2, tq, 1), F32),
                        pltpu.VMEM((2, tq, 1), F32),
                        pltpu.VMEM((2, tq, ATTN_V_DIM), F32)],
        compiler_params=_params("parallel", "parallel", "parallel"),
        name="diff_attn",
    )(qk, qk, z, lam_qk, attn_norm_g.reshape(1, ATTN_V_DIM))


def _gelu(x):
    return 0.5 * x * (1.0 + lax.erf(x * (2.0 ** -0.5)))


def _sgu_kernel(zu_ref, zv_ref, g_ref, b_ref, w_ref, bt_ref, o_ref, *, tm):
    sv = _gelu(zv_ref[...].astype(F32))
    mu = jnp.mean(sv, axis=-1, keepdims=True)
    var = jnp.mean(jnp.square(sv - mu), axis=-1, keepdims=True)
    sv = ((sv - mu) * lax.rsqrt(var + LN_EPS) * g_ref[...] + b_ref[...]).astype(BF16)
    row = lax.broadcasted_iota(jnp.int32, (SGU_CHUNK, SGU_CHUNK), 0)
    col = lax.broadcasted_iota(jnp.int32, (SGU_CHUNK, SGU_CHUNK), 1)
    bt = bt_ref[...]
    for g in range(SGU_GROUPS):
        cols = slice(g * V7X_LANES, (g + 1) * V7X_LANES)
        ws = jnp.where(col <= row, w_ref[g], 0.0).astype(BF16)
        bias = bt[:, g:g + 1]
        for ch in range(tm // SGU_CHUNK):
            rows = slice(ch * SGU_CHUNK, (ch + 1) * SGU_CHUNK)
            sp = _dot(ws, sv[rows, cols]) + bias
            u = _gelu(zu_ref[rows, cols].astype(F32))
            o_ref[rows, cols] = (u * sp).astype(o_ref.dtype)


def _sgu(z, ln_g, ln_b, sgu_w, sgu_b):
    m = z.shape[0]
    tm = 512
    ucol = COL_SGU // SGU_WIDTH
    return pl.pallas_call(
        functools.partial(_sgu_kernel, tm=tm),
        grid=(m // tm,),
        in_specs=[pl.BlockSpec((tm, SGU_WIDTH), lambda i: (i, ucol)),
                  pl.BlockSpec((tm, SGU_WIDTH), lambda i: (i, ucol + 1)),
                  pl.BlockSpec((1, SGU_WIDTH), lambda i: (0, 0)),
                  pl.BlockSpec((1, SGU_WIDTH), lambda i: (0, 0)),
                  pl.BlockSpec((SGU_GROUPS, SGU_CHUNK, SGU_CHUNK), lambda i: (0, 0, 0)),
                  pl.BlockSpec((SGU_CHUNK, SGU_GROUPS), lambda i: (0, 0))],
        out_specs=pl.BlockSpec((tm, SGU_WIDTH), lambda i: (i, 0)),
        out_shape=jax.ShapeDtypeStruct((m, SGU_WIDTH), BF16),
        compiler_params=_params("parallel"),
        name="sgu",
    )(z, z, ln_g.reshape(1, -1), ln_b.reshape(1, -1), sgu_w, sgu_b.T)


def _conv_kernel(za_ref, zb_ref, w_ref, b_ref, g_ref, beta_ref, o_ref, ybuf, yconv, *, tm, per_batch):
    i = pl.program_id(0)

    @pl.when(i % per_batch == 0)
    def _():
        ybuf[0:HALO, :] = jnp.zeros((HALO, CONV_WIDTH), F32)

    ybuf[HALO:HALO + tm, :] = za_ref[...].astype(F32) * _sigmoid(zb_ref[...].astype(F32))

    rc = 128
    first = HALO - (CONV_KERNEL - 1)
    for c in range(CONV_WIDTH // V7X_LANES):
        cols = slice(c * V7X_LANES, (c + 1) * V7X_LANES)
        w = w_ref[:, cols]
        for r in range(tm // rc):
            acc = jnp.broadcast_to(b_ref[:, cols], (rc, V7X_LANES))
            for j in range(CONV_KERNEL):
                acc = acc + w[j:j + 1, :] * ybuf[r * rc + first + j:r * rc + first + j + rc, cols]
            yconv[r * rc:(r + 1) * rc, cols] = acc

    ybuf[0:HALO, :] = ybuf[tm:tm + HALO, :]

    y = yconv[...]
    mu = jnp.mean(y, axis=-1, keepdims=True)
    var = jnp.mean(jnp.square(y - mu), axis=-1, keepdims=True)
    yn = (y - mu) * lax.rsqrt(var + LN_EPS) * g_ref[...] + beta_ref[...]
    o_ref[...] = (yn * _sigmoid(yn)).astype(o_ref.dtype)


def _conv_branch(z, w, b, ln_g, ln_b, seq):
    m = z.shape[0]
    tm = 512
    acol = COL_CONV // CONV_WIDTH
    return pl.pallas_call(
        functools.partial(_conv_kernel, tm=tm, per_batch=seq // tm),
        grid=(m // tm,),
        in_specs=[pl.BlockSpec((tm, CONV_WIDTH), lambda i: (i, acol)),
                  pl.BlockSpec((tm, CONV_WIDTH), lambda i: (i, acol + 1)),
                  pl.BlockSpec((CONV_KERNEL, CONV_WIDTH), lambda i: (0, 0)),
                  pl.BlockSpec((1, CONV_WIDTH), lambda i: (0, 0)),
                  pl.BlockSpec((1, CONV_WIDTH), lambda i: (0, 0)),
                  pl.BlockSpec((1, CONV_WIDTH), lambda i: (0, 0))],
        out_specs=pl.BlockSpec((tm, CONV_WIDTH), lambda i: (i, 0)),
        out_shape=jax.ShapeDtypeStruct((m, CONV_WIDTH), BF16),
        scratch_shapes=[pltpu.VMEM((HALO + tm, CONV_WIDTH), F32),
                        pltpu.VMEM((tm, CONV_WIDTH), F32)],
        compiler_params=_params("arbitrary"),
        name="conv_branch",
    )(z, z, w, b.reshape(1, -1), ln_g.reshape(1, -1), ln_b.reshape(1, -1))


def _merge_kernel(oa_ref, os_ref, oc_ref, wa_ref, ws_ref, wc_ref,
                  ga_ref, gs_ref, gc_ref, bg_ref, o_ref):
    def gate(z_ref, idx):
        return _sigmoid(z_ref[...].astype(F32) + bg_ref[idx:idx + 1, :])

    acc = gate(ga_ref, 0) * _dot(oa_ref[...], wa_ref[...])
    acc = acc + gate(gs_ref, 1) * _dot(os_ref[...], ws_ref[...])
    acc = acc + gate(gc_ref, 2) * _dot(oc_ref[...], wc_ref[...])
    o_ref[...] = acc.astype(o_ref.dtype)


def _merge(o_attn, o_sgu, o_conv, wpa, wps, wpc, z, b_gates3):
    m = z.shape[0]
    d = D_MODEL
    tm, tn = 512, 1024
    gcol0 = COL_GATE // tn
    nb = d // tn
    return pl.pallas_call(
        _merge_kernel,
        grid=(d // tn, m // tm),
        in_specs=[pl.BlockSpec((tm, ATTN_WIDTH), lambda j, i: (i, 0)),
                  pl.BlockSpec((tm, SGU_WIDTH), lambda j, i: (i, 0)),
                  pl.BlockSpec((tm, CONV_WIDTH), lambda j, i: (i, 0)),
                  pl.BlockSpec((ATTN_WIDTH, tn), lambda j, i: (0, j)),
                  pl.BlockSpec((SGU_WIDTH, tn), lambda j, i: (0, j)),
                  pl.BlockSpec((CONV_WIDTH, tn), lambda j, i: (0, j)),
                  pl.BlockSpec((tm, tn), lambda j, i: (i, gcol0 + j)),
                  pl.BlockSpec((tm, tn), lambda j, i: (i, gcol0 + nb + j)),
                  pl.BlockSpec((tm, tn), lambda j, i: (i, gcol0 + 2 * nb + j)),
                  pl.BlockSpec((3, tn), lambda j, i: (0, j))],
        out_specs=pl.BlockSpec((tm, tn), lambda j, i: (i, j)),
        out_shape=jax.ShapeDtypeStruct((m, d), BF16),
        compiler_params=_params("parallel", "parallel"),
        name="merge",
    )(o_attn, o_sgu, o_conv, wpa, wps, wpc, z, z, z, b_gates3)


def _proj_res_kernel(a_ref, w_ref, x_ref, mod_ref, o_ref, *, gate_row):
    gate = mod_ref[gate_row:gate_row + 1, :]
    o_ref[...] = x_ref[...] + gate * _dot(a_ref[...], w_ref[...])


def _proj_residual(a, w, x, mod, seq, *, gate_row):
    m, k = a.shape
    d = w.shape[1]
    tm, tn = 1024, 512
    per_batch = seq // tm
    return pl.pallas_call(
        functools.partial(_proj_res_kernel, gate_row=gate_row),
        grid=(m // tm, d // tn),
        in_specs=[pl.BlockSpec((tm, k), lambda i, j: (i, 0)),
                  pl.BlockSpec((k, tn), lambda i, j: (0, j)),
                  pl.BlockSpec((tm, tn), lambda i, j: (i, j)),
                  pl.BlockSpec((None, 6, tn), lambda i, j: (i // per_batch, 0, j))],
        out_specs=pl.BlockSpec((tm, tn), lambda i, j: (i, j)),
        out_shape=jax.ShapeDtypeStruct((m, d), F32),
        compiler_params=_params("parallel", "parallel"),
        name="proj_residual",
    )(a, w, x, mod)


def _ffn_up_kernel(h_ref, wa_ref, wb_ref, cw_ref, cb_ref, o_ref, fbuf, *, tm, per_batch):
    i = pl.program_id(1)
    h = h_ref[...]

    @pl.when(i % per_batch == 0)
    def _():
        fbuf[0:V7X_SUBLANES, :] = jnp.zeros((V7X_SUBLANES, fbuf.shape[1]), F32)

    fbuf[V7X_SUBLANES:V7X_SUBLANES + tm, :] = _dot(h, wa_ref[...])
    cw = cw_ref[...]
    y = cb_ref[...] + cw[2:3, :] * fbuf[V7X_SUBLANES:V7X_SUBLANES + tm, :]
    y = y + cw[1:2, :] * fbuf[V7X_SUBLANES - 1:V7X_SUBLANES - 1 + tm, :]
    y = y + cw[0:1, :] * fbuf[V7X_SUBLANES - 2:V7X_SUBLANES - 2 + tm, :]
    fbuf[0:V7X_SUBLANES, :] = fbuf[tm:tm + V7X_SUBLANES, :]
    fb = _dot(h, wb_ref[...])
    o_ref[...] = (y * _sigmoid(y) * fb).astype(o_ref.dtype)


def _ffn_up(h, wa, wb, cw, cb, seq):
    m, d = h.shape
    n = wa.shape[1]
    tm, tn = 1024, 512
    return pl.pallas_call(
        functools.partial(_ffn_up_kernel, tm=tm, per_batch=seq // tm),
        grid=(n // tn, m // tm),
        in_specs=[pl.BlockSpec((tm, d), lambda j, i: (i, 0)),
                  pl.BlockSpec((d, tn), lambda j, i: (0, j)),
                  pl.BlockSpec((d, tn), lambda j, i: (0, j)),
                  pl.BlockSpec((FFN_CONV_KERNEL, tn), lambda j, i: (0, j)),
                  pl.BlockSpec((1, tn), lambda j, i: (0, j))],
        out_specs=pl.BlockSpec((tm, tn), lambda j, i: (i, j)),
        out_shape=jax.ShapeDtypeStruct((m, n), BF16),
        scratch_shapes=[pltpu.VMEM((V7X_SUBLANES + tm, tn), F32)],
        compiler_params=_params("arbitrary", "arbitrary"),
        name="ffn_up",
    )(h, wa, wb, cw, cb)


def _ffn_down_kernel(a_ref, w_ref, x_ref, mod_ref, o_ref, *, gate_row, nk):
    k = pl.program_id(2)
    part = _dot(a_ref[...], w_ref[...])

    @pl.when(k == 0)
    def _():
        o_ref[...] = part

    @pl.when(jnp.logical_and(k > 0, k < nk - 1))
    def _():
        o_ref[...] += part

    @pl.when(k == nk - 1)
    def _():
        gate = mod_ref[gate_row:gate_row + 1, :]
        o_ref[...] = x_ref[...] + gate * (o_ref[...] + part)


def _ffn_down(a, w, x, mod, seq, *, gate_row):
    m, kdim = a.shape
    d = w.shape[1]
    tm, tn, tk = 1024, 1024, 2816
    nk = kdim // tk
    per_batch = seq // tm
    return pl.pallas_call(
        functools.partial(_ffn_down_kernel, gate_row=gate_row, nk=nk),
        grid=(m // tm, d // tn, nk),
        in_specs=[pl.BlockSpec((tm, tk), lambda i, j, k: (i, k)),
                  pl.BlockSpec((tk, tn), lambda i, j, k: (k, j)),
                  pl.BlockSpec((tm, tn), lambda i, j, k: (i, j)),
                  pl.BlockSpec((None, 6, tn), lambda i, j, k: (i // per_batch, 0, j))],
        out_specs=pl.BlockSpec((tm, tn), lambda i, j, k: (i, j)),
        out_shape=jax.ShapeDtypeStruct((m, d), F32),
        compiler_params=_params("parallel", "parallel", "arbitrary"),
        name="ffn_down",
    )(a, w, x, mod)


def kernel(x, c, ada_w, ada_b, norm1_g, w_in, b_gates, lam_qk, attn_norm_g, sgu_ln_g, sgu_ln_b, sgu_w, sgu_b, conv_dw_w, conv_dw_b, conv_ln_g, conv_ln_b, w_proj_attn, w_proj_sgu, w_proj_conv, w_out, norm2_g, ffn_up, ffn_dw_w, ffn_dw_b, ffn_down, final_g):
    batch, seq, d = x.shape
    depth = ada_w.shape[0]
    assert (d, seq % 1024) == (D_MODEL, 0)
    m = batch * seq
    xf = x.reshape(m, d)

    c_pad = jnp.zeros((V7X_SUBLANES, d), F32).at[:batch].set(c)
    mod_all = _ada_mod(c_pad, ada_w, ada_b)

    half = HEAD_DIM // 2
    inv = ROPE_THETA ** (-jnp.arange(half, dtype=F32) / half)
    ang = jnp.arange(seq, dtype=F32)[:, None] * inv[None, :]
    cos_t = jnp.concatenate([jnp.cos(ang), jnp.cos(ang)], axis=-1)
    sin_t = jnp.concatenate([-jnp.sin(ang), jnp.sin(ang)], axis=-1)

    pad = FFN_PAD - FFN_DIM
    for l in range(depth):
        mod = mod_all[l, :batch].reshape(batch, 6, d)
        lam_init = 0.8 - 0.6 * math.exp(-0.3 * l)

        h = _norm_mod(xf, norm1_g[l], mod, seq, shift_row=0, scale_row=1)
        z = _matmul(h, w_in[l].astype(BF16), tm=1024, tn=1024, out_dtype=BF16, name="in_proj")
        qk = _rope(z, cos_t, sin_t, seq)
        o_attn = _attention(qk, z, lam_qk[l], attn_norm_g[l], batch, seq, lam_init)
        o_sgu = _sgu(z, sgu_ln_g[l], sgu_ln_b[l], sgu_w[l], sgu_b[l])
        o_conv = _conv_branch(z, conv_dw_w[l], conv_dw_b[l], conv_ln_g[l], conv_ln_b[l], seq)
        merged = _merge(o_attn, o_sgu, o_conv, w_proj_attn[l].astype(BF16), w_proj_sgu[l].astype(BF16),
                        w_proj_conv[l].astype(BF16), z, b_gates[l].reshape(3, d))
        xf = _proj_residual(merged, w_out[l].astype(BF16), xf, mod, seq, gate_row=2)

        h2 = _norm_mod(xf, norm2_g[l], mod, seq, shift_row=3, scale_row=4)
        wa = jnp.pad(ffn_up[l, :, :FFN_DIM].astype(BF16), ((0, 0), (0, pad)))
        wb = jnp.pad(ffn_up[l, :, FFN_DIM:].astype(BF16), ((0, 0), (0, pad)))
        cw = jnp.pad(ffn_dw_w[l], ((0, 0), (0, pad)))
        cb = jnp.pad(ffn_dw_b[l], (0, pad)).reshape(1, FFN_PAD)
        act = _ffn_up(h2, wa, wb, cw, cb, seq)
        wd = jnp.pad(ffn_down[l].astype(BF16), ((0, pad), (0, 0)))
        xf = _ffn_down(act, wd, xf, mod, seq, gate_row=5)

    return _final_norm(xf, final_g).reshape(batch, seq, d)
```

```python
import functools
import math

import jax
import jax.numpy as jnp
from jax import lax
from jax.experimental import pallas as pl
from jax.experimental.pallas import tpu as pltpu

D_MODEL = 4096
HEAD_DIM = 128
ATTN_V_DIM = 2 * HEAD_DIM
ATTN_WIDTH = D_MODEL // 2
ATTN_HEADS = ATTN_WIDTH // ATTN_V_DIM
QK_WIDTH = ATTN_HEADS * 2 * HEAD_DIM
ROPE_THETA = 10000.0
SGU_WIDTH = D_MODEL // 4
SGU_CHUNK = 128
SGU_GROUPS = 8
CONV_WIDTH = D_MODEL // 4
CONV_KERNEL = 31
FFN_DIM = ((8 * D_MODEL // 3 + 255) // 256) * 256
FFN_CONV_KERNEL = 3
NORM_EPS = 1e-6
LN_EPS = 1e-5

COL_Q = 0
COL_K = QK_WIDTH
COL_V = 2 * QK_WIDTH
COL_SGU = COL_V + ATTN_WIDTH
COL_CONV = COL_SGU + 2 * SGU_WIDTH
COL_GATE = COL_CONV + 2 * CONV_WIDTH
IN_COLS = COL_GATE + 3 * D_MODEL

V7X_LANES = 128
V7X_SUBLANES = 8
V7X_VMEM_LIMIT_BYTES = 52 * 1024 * 1024
FFN_PAD = 11264
HALO = 32
NEG_BIG = -1e30

BF16 = jnp.bfloat16
F32 = jnp.float32


def _params(*sem):
    return pltpu.CompilerParams(dimension_semantics=sem, vmem_limit_bytes=V7X_VMEM_LIMIT_BYTES)


def _sigmoid(x):
    return 1.0 / (1.0 + jnp.exp(-x))


def _dot(a, b):
    return jnp.dot(a, b, preferred_element_type=F32)


def _cast_kernel(w_ref, o_ref):
    o_ref[...] = w_ref[...].astype(o_ref.dtype)


def _cast_weight(w_stack, l, *, tr, tn):
    _, r, n = w_stack.shape
    return pl.pallas_call(
        _cast_kernel,
        grid=(r // tr, n // tn),
        in_specs=[pl.BlockSpec((None, tr, tn), lambda i, j: (l, i, j))],
        out_specs=pl.BlockSpec((tr, tn), lambda i, j: (i, j)),
        out_shape=jax.ShapeDtypeStruct((r, n), BF16),
        compiler_params=_params("parallel", "parallel"),
        name="cast_weight",
    )(w_stack)


def _cast_ffn_up_kernel(a_ref, b_ref, oa_ref, ob_ref):
    n = a_ref.shape[1]
    pad = oa_ref.shape[1] - n
    for src, dst in ((a_ref, oa_ref), (b_ref, ob_ref)):
        dst[:, 0:n] = src[...].astype(dst.dtype)
        dst[:, n:n + pad] = jnp.zeros((dst.shape[0], pad), dst.dtype)


def _cast_ffn_up(ffn_up, l):
    _, d, _ = ffn_up.shape
    tr = 64
    return pl.pallas_call(
        _cast_ffn_up_kernel,
        grid=(d // tr,),
        in_specs=[pl.BlockSpec((None, tr, FFN_DIM), lambda i: (l, i, 0)),
                  pl.BlockSpec((None, tr, FFN_DIM), lambda i: (l, i, 1))],
        out_specs=[pl.BlockSpec((tr, FFN_PAD), lambda i: (i, 0)),
                   pl.BlockSpec((tr, FFN_PAD), lambda i: (i, 0))],
        out_shape=[jax.ShapeDtypeStruct((d, FFN_PAD), BF16)] * 2,
        compiler_params=_params("parallel"),
        name="cast_ffn_up",
    )(ffn_up, ffn_up)


def _cast_ffn_down_kernel(w_ref, o_ref, *, n_real):
    i = pl.program_id(0)

    @pl.when(i < n_real)
    def _():
        o_ref[...] = w_ref[...].astype(o_ref.dtype)

    @pl.when(i >= n_real)
    def _():
        o_ref[...] = jnp.zeros(o_ref.shape, o_ref.dtype)


def _cast_ffn_down(ffn_down, l):
    _, _, d = ffn_down.shape
    tr = FFN_PAD - FFN_DIM
    n_real = FFN_DIM // tr
    return pl.pallas_call(
        functools.partial(_cast_ffn_down_kernel, n_real=n_real),
        grid=(FFN_PAD // tr,),
        in_specs=[pl.BlockSpec((None, tr, d), lambda i: (l, jnp.minimum(i, n_real - 1), 0))],
        out_specs=pl.BlockSpec((tr, d), lambda i: (i, 0)),
        out_shape=jax.ShapeDtypeStruct((FFN_PAD, d), BF16),
        compiler_params=_params("parallel"),
        name="cast_ffn_down",
    )(ffn_down)


def _ada_kernel(c_ref, w_ref, b_ref, o_ref):
    c = c_ref[...]
    c_act = c * _sigmoid(c)
    o_ref[...] = _dot(c_act.astype(BF16), w_ref[...].astype(BF16)) + b_ref[...]


def _ada_mod(c_pad, ada_w, ada_b):
    depth, d, n = ada_w.shape
    tn = 512
    rows = c_pad.shape[0]
    return pl.pallas_call(
        _ada_kernel,
        grid=(depth, n // tn),
        in_specs=[pl.BlockSpec((rows, d), lambda l, j: (0, 0)),
                  pl.BlockSpec((None, d, tn), lambda l, j: (l, 0, j)),
                  pl.BlockSpec((None, 1, tn), lambda l, j: (l, 0, j))],
        out_specs=pl.BlockSpec((None, rows, tn), lambda l, j: (l, 0, j)),
        out_shape=jax.ShapeDtypeStruct((depth, rows, n), F32),
        compiler_params=_params("parallel", "parallel"),
        name="ada_mod",
    )(c_pad, ada_w, ada_b.reshape(depth, 1, n))


def _norm_mod_kernel(x_ref, g_ref, mod_ref, o_ref, *, shift_row, scale_row):
    x = x_ref[...]
    y = x * lax.rsqrt(jnp.mean(x * x, axis=-1, keepdims=True) + NORM_EPS) * g_ref[...]
    sc = mod_ref[scale_row:scale_row + 1, :]
    sh = mod_ref[shift_row:shift_row + 1, :]
    o_ref[...] = (y * (1.0 + sc) + sh).astype(o_ref.dtype)


def _norm_mod(x, g, mod, seq, *, shift_row, scale_row):
    m, d = x.shape
    tm = 256
    per_batch = seq // tm
    return pl.pallas_call(
        functools.partial(_norm_mod_kernel, shift_row=shift_row, scale_row=scale_row),
        grid=(m // tm,),
        in_specs=[pl.BlockSpec((tm, d), lambda i: (i, 0)),
                  pl.BlockSpec((1, d), lambda i: (0, 0)),
                  pl.BlockSpec((None, 6, d), lambda i: (i // per_batch, 0, 0))],
        out_specs=pl.BlockSpec((tm, d), lambda i: (i, 0)),
        out_shape=jax.ShapeDtypeStruct((m, d), BF16),
        compiler_params=_params("parallel"),
        name="norm_mod",
    )(x, g.reshape(1, d), mod)


def _final_norm_kernel(x_ref, g_ref, o_ref):
    x = x_ref[...]
    o_ref[...] = x * lax.rsqrt(jnp.mean(x * x, axis=-1, keepdims=True) + NORM_EPS) * g_ref[...]


def _final_norm(x, g):
    m, d = x.shape
    tm = 256
    return pl.pallas_call(
        _final_norm_kernel,
        grid=(m // tm,),
        in_specs=[pl.BlockSpec((tm, d), lambda i: (i, 0)),
                  pl.BlockSpec((1, d), lambda i: (0, 0))],
        out_specs=pl.BlockSpec((tm, d), lambda i: (i, 0)),
        out_shape=jax.ShapeDtypeStruct((m, d), F32),
        compiler_params=_params("parallel"),
        name="final_norm",
    )(x, g.reshape(1, d))


def _mm_kernel(a_ref, b_ref, o_ref):
    o_ref[...] = _dot(a_ref[...], b_ref[...]).astype(o_ref.dtype)


def _matmul(a, b, *, tm, tn, out_dtype, name):
    m, k = a.shape
    _, n = b.shape
    return pl.pallas_call(
        _mm_kernel,
        grid=(m // tm, n // tn),
        in_specs=[pl.BlockSpec((tm, k), lambda i, j: (i, 0)),
                  pl.BlockSpec((k, tn), lambda i, j: (0, j))],
        out_specs=pl.BlockSpec((tm, tn), lambda i, j: (i, j)),
        out_shape=jax.ShapeDtypeStruct((m, n), out_dtype),
        compiler_params=_params("parallel", "parallel"),
        name=name,
    )(a, b)


def _rotary(x, cos, sin):
    return x * cos + pltpu.roll(x, HEAD_DIM // 2, axis=1) * sin


def _attn_kernel(q_ref, k_ref, v_ref, cos_ref, sin_ref, lamqk_ref, g_ref, o_ref,
                 kt_ref, qd_ref, sa_ref, sb_ref, m_ref, l_ref, acc_ref, *, tq, tk, rc, lam_init):
    i = pl.program_id(2)
    seq = k_ref.shape[0]
    rows2 = 2 * tq
    n_lane_groups = tk // V7X_LANES

    @pl.when(i == 0)
    def _():
        def rope_rows(c, carry):
            r0 = pl.multiple_of(c * tk, tk)
            cos = cos_ref[pl.ds(r0, tk), :]
            sin = sin_ref[pl.ds(r0, tk), :]
            for g in range(2):
                x = k_ref[pl.ds(r0, tk), g * HEAD_DIM:(g + 1) * HEAD_DIM].astype(F32)
                kt_ref[c, g * HEAD_DIM:(g + 1) * HEAD_DIM, :] = _rotary(x, cos, sin).T.astype(BF16)
            return carry
        lax.fori_loop(0, seq // tk, rope_rows, 0)

    q0 = pl.multiple_of(i * tq, tq)
    cos_q = cos_ref[pl.ds(q0, tq), :]
    sin_q = sin_ref[pl.ds(q0, tq), :]
    q_scale = HEAD_DIM ** -0.5 * math.log2(math.e)
    zero = jnp.zeros((tq, HEAD_DIM), BF16)
    for g in range(2):
        x = q_ref[:, g * HEAD_DIM:(g + 1) * HEAD_DIM].astype(F32)
        qd_ref[g * tq:(g + 1) * tq, g * HEAD_DIM:(g + 1) * HEAD_DIM] = (_rotary(x, cos_q, sin_q) * q_scale).astype(BF16)
        qd_ref[g * tq:(g + 1) * tq, (1 - g) * HEAD_DIM:(2 - g) * HEAD_DIM] = zero

    m_ref[...] = jnp.full(m_ref.shape, NEG_BIG, F32)
    l_ref[...] = jnp.zeros(l_ref.shape, F32)
    acc_ref[...] = jnp.zeros(acc_ref.shape, F32)

    def scores(kb, s_ref):
        s_ref[...] = _dot(qd_ref[...], kt_ref[kb])

    def update(kb, s_ref, masked):
        k0 = pl.multiple_of(kb * tk, tk)
        v = v_ref[pl.ds(k0, tk), :]
        for c in range(rows2 // rc):
            rows = slice(c * rc, (c + 1) * rc)
            sg = [s_ref[rows, g * V7X_LANES:(g + 1) * V7X_LANES] for g in range(n_lane_groups)]
            if masked:
                row = lax.broadcasted_iota(jnp.int32, (rc, V7X_LANES), 0) + (c * rc) % tq
                col = lax.broadcasted_iota(jnp.int32, (rc, V7X_LANES), 1)
                sg = [jnp.where(col + g * V7X_LANES <= row, x, NEG_BIG) for g, x in enumerate(sg)]
            mx = sg[0]
            for x in sg[1:]:
                mx = jnp.maximum(mx, x)
            m_prev = m_ref[rows, :]
            m_new = jnp.maximum(m_prev, jnp.max(mx, axis=-1, keepdims=True))
            alpha = jnp.exp2(m_prev - m_new)
            ps = [jnp.exp2(x - m_new) for x in sg]
            lsum = ps[0]
            for x in ps[1:]:
                lsum = lsum + x
            l_ref[rows, :] = alpha * l_ref[rows, :] + lsum
            p = jnp.concatenate([x.astype(BF16) for x in ps], axis=1)
            acc_ref[rows, :] = jnp.concatenate([alpha, alpha], axis=1) * acc_ref[rows, :] + _dot(p, v)
            m_ref[rows, :] = m_new

    scores(0, sa_ref)

    def pair(t, carry):
        scores(2 * t + 1, sb_ref)
        update(2 * t, sa_ref, False)
        scores(2 * t + 2, sa_ref)
        update(2 * t + 1, sb_ref, False)
        return carry
    lax.fori_loop(0, i // 2, pair, 0)

    @pl.when(i % 2 == 0)
    def _():
        update(i, sa_ref, True)

    @pl.when(i % 2 == 1)
    def _():
        scores(i, sb_ref)
        update(i - 1, sa_ref, False)
        update(i, sb_ref, True)

    lq = lamqk_ref[...]
    lam = (jnp.exp(jnp.sum(lq[0:1, :] * lq[1:2, :], axis=-1, keepdims=True))
           - jnp.exp(jnp.sum(lq[2:3, :] * lq[3:4, :], axis=-1, keepdims=True)) + lam_init)
    inv_l = 1.0 / jnp.sum(l_ref[...], axis=-1, keepdims=True)
    o = acc_ref[0:tq, :] * inv_l[0:tq] - lam * (acc_ref[tq:rows2, :] * inv_l[tq:rows2])
    y = o * lax.rsqrt(jnp.mean(o * o, axis=-1, keepdims=True) + NORM_EPS) * g_ref[...]
    o_ref[...] = (y * (1.0 - lam_init)).astype(o_ref.dtype)


def _attention(z, cos_t, sin_t, lam_qk, attn_norm_g, batch, seq, lam_init):
    m = z.shape[0]
    tq = tk = 512
    nq = seq // tq
    kcol0 = COL_K // ATTN_V_DIM
    vcol0 = COL_V // ATTN_V_DIM
    return pl.pallas_call(
        functools.partial(_attn_kernel, tq=tq, tk=tk, rc=128, lam_init=lam_init),
        grid=(batch, ATTN_HEADS, nq),
        in_specs=[pl.BlockSpec((tq, ATTN_V_DIM), lambda b, h, i: (b * nq + i, h)),
                  pl.BlockSpec((seq, ATTN_V_DIM), lambda b, h, i: (b, kcol0 + h)),
                  pl.BlockSpec((seq, ATTN_V_DIM), lambda b, h, i: (b, vcol0 + h)),
                  pl.BlockSpec((seq, HEAD_DIM), lambda b, h, i: (0, 0)),
                  pl.BlockSpec((seq, HEAD_DIM), lambda b, h, i: (0, 0)),
                  pl.BlockSpec((4, HEAD_DIM), lambda b, h, i: (0, 0)),
                  pl.BlockSpec((1, ATTN_V_DIM), lambda b, h, i: (0, 0))],
        out_specs=pl.BlockSpec((tq, ATTN_V_DIM), lambda b, h, i: (b * nq + i, h)),
        out_shape=jax.ShapeDtypeStruct((m, ATTN_WIDTH), BF16),
        scratch_shapes=[pltpu.VMEM((seq // tk, ATTN_V_DIM, tk), BF16),
                        pltpu.VMEM((2 * tq, ATTN_V_DIM), BF16),
                        pltpu.VMEM((2 * tq, tk), F32),
                        pltpu.VMEM((2 * tq, tk), F32),
                        pltpu.VMEM((2 * tq, V7X_LANES), F32),
                        pltpu.VMEM((2 * tq, V7X_LANES), F32),
                        pltpu.VMEM((2 * tq, ATTN_V_DIM), F32)],
        compiler_params=_params("arbitrary", "arbitrary", "arbitrary"),
        name="diff_attn",
    )(z, z, z, cos_t, sin_t, lam_qk, attn_norm_g.reshape(1, ATTN_V_DIM))


def _gelu(x):
    return 0.5 * x * (1.0 + lax.erf(x * (2.0 ** -0.5)))


def _sgu_kernel(zu_ref, zv_ref, g_ref, b_ref, w_ref, bt_ref, o_ref, *, tm):
    sv = _gelu(zv_ref[...].astype(F32))
    mu = jnp.mean(sv, axis=-1, keepdims=True)
    var = jnp.mean(jnp.square(sv - mu), axis=-1, keepdims=True)
    sv = ((sv - mu) * lax.rsqrt(var + LN_EPS) * g_ref[...] + b_ref[...]).astype(BF16)
    row = lax.broadcasted_iota(jnp.int32, (SGU_CHUNK, SGU_CHUNK), 0)
    col = lax.broadcasted_iota(jnp.int32, (SGU_CHUNK, SGU_CHUNK), 1)
    bt = bt_ref[...]
    for g in range(SGU_GROUPS):
        cols = slice(g * V7X_LANES, (g + 1) * V7X_LANES)
        ws = jnp.where(col <= row, w_ref[g], 0.0).astype(BF16)
        bias = bt[:, g:g + 1]
        for ch in range(tm // SGU_CHUNK):
            rows = slice(ch * SGU_CHUNK, (ch + 1) * SGU_CHUNK)
            sp = _dot(ws, sv[rows, cols]) + bias
            u = _gelu(zu_ref[rows, cols].astype(F32))
            o_ref[rows, cols] = (u * sp).astype(o_ref.dtype)


def _sgu(z, ln_g, ln_b, sgu_w, sgu_b):
    m = z.shape[0]
    tm = 512
    ucol = COL_SGU // SGU_WIDTH
    return pl.pallas_call(
        functools.partial(_sgu_kernel, tm=tm),
        grid=(m // tm,),
        in_specs=[pl.BlockSpec((tm, SGU_WIDTH), lambda i: (i, ucol)),
                  pl.BlockSpec((tm, SGU_WIDTH), lambda i: (i, ucol + 1)),
                  pl.BlockSpec((1, SGU_WIDTH), lambda i: (0, 0)),
                  pl.BlockSpec((1, SGU_WIDTH), lambda i: (0, 0)),
                  pl.BlockSpec((SGU_GROUPS, SGU_CHUNK, SGU_CHUNK), lambda i: (0, 0, 0)),
                  pl.BlockSpec((SGU_CHUNK, SGU_GROUPS), lambda i: (0, 0))],
        out_specs=pl.BlockSpec((tm, SGU_WIDTH), lambda i: (i, 0)),
        out_shape=jax.ShapeDtypeStruct((m, SGU_WIDTH), BF16),
        compiler_params=_params("parallel"),
        name="sgu",
    )(z, z, ln_g.reshape(1, -1), ln_b.reshape(1, -1), sgu_w, sgu_b.T)


def _conv_kernel(za_ref, zb_ref, w_ref, b_ref, g_ref, beta_ref, o_ref, ybuf, yconv, *, tm, per_batch):
    i = pl.program_id(0)

    @pl.when(i % per_batch == 0)
    def _():
        ybuf[0:HALO, :] = jnp.zeros((HALO, CONV_WIDTH), F32)

    ybuf[HALO:HALO + tm, :] = za_ref[...].astype(F32) * _sigmoid(zb_ref[...].astype(F32))

    rc = 128
    for c in range(CONV_WIDTH // V7X_LANES):
        cols = slice(c * V7X_LANES, (c + 1) * V7X_LANES)
        w = w_ref[:, cols]
        for r in range(tm // rc):
            base = HALO + r * rc - V7X_SUBLANES
            acc = jnp.broadcast_to(b_ref[:, cols], (rc, V7X_LANES))
            for b in range(V7X_SUBLANES):
                u = None
                for k in range(b, CONV_KERNEL, V7X_SUBLANES):
                    tap = CONV_KERNEL - 1 - k
                    term = w[tap:tap + 1, :] * ybuf[base - (k - b):base - (k - b) + rc + V7X_SUBLANES, cols]
                    u = term if u is None else u + term
                acc = acc + u[V7X_SUBLANES - b:V7X_SUBLANES - b + rc, :]
            yconv[r * rc:(r + 1) * rc, cols] = acc

    ybuf[0:HALO, :] = ybuf[tm:tm + HALO, :]

    y = yconv[...]
    mu = jnp.mean(y, axis=-1, keepdims=True)
    var = jnp.mean(jnp.square(y - mu), axis=-1, keepdims=True)
    yn = (y - mu) * lax.rsqrt(var + LN_EPS) * g_ref[...] + beta_ref[...]
    o_ref[...] = (yn * _sigmoid(yn)).astype(o_ref.dtype)


def _conv_branch(z, w, b, ln_g, ln_b, seq):
    m = z.shape[0]
    tm = 512
    acol = COL_CONV // CONV_WIDTH
    return pl.pallas_call(
        functools.partial(_conv_kernel, tm=tm, per_batch=seq // tm),
        grid=(m // tm,),
        in_specs=[pl.BlockSpec((tm, CONV_WIDTH), lambda i: (i, acol)),
                  pl.BlockSpec((tm, CONV_WIDTH), lambda i: (i, acol + 1)),
                  pl.BlockSpec((CONV_KERNEL, CONV_WIDTH), lambda i: (0, 0)),
                  pl.BlockSpec((1, CONV_WIDTH), lambda i: (0, 0)),
                  pl.BlockSpec((1, CONV_WIDTH), lambda i: (0, 0)),
                  pl.BlockSpec((1, CONV_WIDTH), lambda i: (0, 0))],
        out_specs=pl.BlockSpec((tm, CONV_WIDTH), lambda i: (i, 0)),
        out_shape=jax.ShapeDtypeStruct((m, CONV_WIDTH), BF16),
        scratch_shapes=[pltpu.VMEM((HALO + tm, CONV_WIDTH), F32),
                        pltpu.VMEM((tm, CONV_WIDTH), F32)],
        compiler_params=_params("arbitrary"),
        name="conv_branch",
    )(z, z, w, b.reshape(1, -1), ln_g.reshape(1, -1), ln_b.reshape(1, -1))


def _merge_kernel(oa_ref, os_ref, oc_ref, wa_ref, ws_ref, wc_ref,
                  ga_ref, gs_ref, gc_ref, bg_ref, o_ref):
    def gate(z_ref, idx):
        return _sigmoid(z_ref[...].astype(F32) + bg_ref[idx:idx + 1, :])

    acc = gate(ga_ref, 0) * _dot(oa_ref[...], wa_ref[...])
    acc = acc + gate(gs_ref, 1) * _dot(os_ref[...], ws_ref[...])
    acc = acc + gate(gc_ref, 2) * _dot(oc_ref[...], wc_ref[...])
    o_ref[...] = acc.astype(o_ref.dtype)


def _merge(o_attn, o_sgu, o_conv, wpa, wps, wpc, z, b_gates3):
    m = z.shape[0]
    d = D_MODEL
    tm, tn = 512, 1024
    gcol0 = COL_GATE // tn
    nb = d // tn
    return pl.pallas_call(
        _merge_kernel,
        grid=(d // tn, m // tm),
        in_specs=[pl.BlockSpec((tm, ATTN_WIDTH), lambda j, i: (i, 0)),
                  pl.BlockSpec((tm, SGU_WIDTH), lambda j, i: (i, 0)),
                  pl.BlockSpec((tm, CONV_WIDTH), lambda j, i: (i, 0)),
                  pl.BlockSpec((ATTN_WIDTH, tn), lambda j, i: (0, j)),
                  pl.BlockSpec((SGU_WIDTH, tn), lambda j, i: (0, j)),
                  pl.BlockSpec((CONV_WIDTH, tn), lambda j, i: (0, j)),
                  pl.BlockSpec((tm, tn), lambda j, i: (i, gcol0 + j)),
                  pl.BlockSpec((tm, tn), lambda j, i: (i, gcol0 + nb + j)),
                  pl.BlockSpec((tm, tn), lambda j, i: (i, gcol0 + 2 * nb + j)),
                  pl.BlockSpec((3, tn), lambda j, i: (0, j))],
        out_specs=pl.BlockSpec((tm, tn), lambda j, i: (i, j)),
        out_shape=jax.ShapeDtypeStruct((m, d), BF16),
        compiler_params=_params("parallel", "parallel"),
        name="merge",
    )(o_attn, o_sgu, o_conv, wpa, wps, wpc, z, z, z, b_gates3)


def _proj_res_kernel(a_ref, w_ref, x_ref, mod_ref, o_ref, *, gate_row):
    gate = mod_ref[gate_row:gate_row + 1, :]
    o_ref[...] = x_ref[...] + gate * _dot(a_ref[...], w_ref[...])


def _proj_residual(a, w, x, mod, seq, *, gate_row):
    m, k = a.shape
    d = w.shape[1]
    tm, tn = 1024, 512
    per_batch = seq // tm
    return pl.pallas_call(
        functools.partial(_proj_res_kernel, gate_row=gate_row),
        grid=(m // tm, d // tn),
        in_specs=[pl.BlockSpec((tm, k), lambda i, j: (i, 0)),
                  pl.BlockSpec((k, tn), lambda i, j: (0, j)),
                  pl.BlockSpec((tm, tn), lambda i, j: (i, j)),
                  pl.BlockSpec((None, 6, tn), lambda i, j: (i // per_batch, 0, j))],
        out_specs=pl.BlockSpec((tm, tn), lambda i, j: (i, j)),
        out_shape=jax.ShapeDtypeStruct((m, d), F32),
        compiler_params=_params("parallel", "parallel"),
        name="proj_residual",
    )(a, w, x, mod)


def _ffn_up_kernel(h_ref, wa_ref, wb_ref, cw_ref, cb_ref, o_ref, fbuf, *, tm, per_batch):
    i = pl.program_id(1)

    @pl.when(i % per_batch == 0)
    def _():
        fbuf[0:V7X_SUBLANES, :] = jnp.zeros((V7X_SUBLANES, fbuf.shape[1]), F32)

    fbuf[V7X_SUBLANES:V7X_SUBLANES + tm, :] = _dot(h_ref[...], wa_ref[...])
    cw = cw_ref[...]
    y = cb_ref[...] + cw[2:3, :] * fbuf[V7X_SUBLANES:V7X_SUBLANES + tm, :]
    y = y + cw[1:2, :] * fbuf[V7X_SUBLANES - 1:V7X_SUBLANES - 1 + tm, :]
    y = y + cw[0:1, :] * fbuf[V7X_SUBLANES - 2:V7X_SUBLANES - 2 + tm, :]
    fbuf[0:V7X_SUBLANES, :] = fbuf[tm:tm + V7X_SUBLANES, :]
    fb = _dot(h_ref[...], wb_ref[...])
    o_ref[...] = (y * _sigmoid(y) * fb).astype(o_ref.dtype)


def _ffn_up(h, wa, wb, cw, cb, seq):
    m, d = h.shape
    n = wa.shape[1]
    tm, tn = 1024, 512
    return pl.pallas_call(
        functools.partial(_ffn_up_kernel, tm=tm, per_batch=seq // tm),
        grid=(n // tn, m // tm),
        in_specs=[pl.BlockSpec((tm, d), lambda j, i: (i, 0)),
                  pl.BlockSpec((d, tn), lambda j, i: (0, j)),
                  pl.BlockSpec((d, tn), lambda j, i: (0, j)),
                  pl.BlockSpec((FFN_CONV_KERNEL, tn), lambda j, i: (0, j)),
                  pl.BlockSpec((1, tn), lambda j, i: (0, j))],
        out_specs=pl.BlockSpec((tm, tn), lambda j, i: (i, j)),
        out_shape=jax.ShapeDtypeStruct((m, n), BF16),
        scratch_shapes=[pltpu.VMEM((V7X_SUBLANES + tm, tn), F32)],
        compiler_params=_params("arbitrary", "arbitrary"),
        name="ffn_up",
    )(h, wa, wb, cw, cb)


def _ffn_down_kernel(a_ref, w_ref, x_ref, mod_ref, o_ref, *, gate_row, nk):
    k = pl.program_id(2)

    @pl.when(k == 0)
    def _():
        o_ref[...] = _dot(a_ref[...], w_ref[...])

    @pl.when(jnp.logical_and(k > 0, k < nk - 1))
    def _():
        o_ref[...] += _dot(a_ref[...], w_ref[...])

    @pl.when(k == nk - 1)
    def _():
        gate = mod_ref[gate_row:gate_row + 1, :]
        o_ref[...] = x_ref[...] + gate * (o_ref[...] + _dot(a_ref[...], w_ref[...]))


def _ffn_down(a, w, x, mod, seq, *, gate_row):
    m, kdim = a.shape
    d = w.shape[1]
    tm, tn, tk = 1024, 1024, 2816
    nk = kdim // tk
    per_batch = seq // tm
    return pl.pallas_call(
        functools.partial(_ffn_down_kernel, gate_row=gate_row, nk=nk),
        grid=(m // tm, d // tn, nk),
        in_specs=[pl.BlockSpec((tm, tk), lambda i, j, k: (i, k)),
                  pl.BlockSpec((tk, tn), lambda i, j, k: (k, j)),
                  pl.BlockSpec((tm, tn), lambda i, j, k: (i, j)),
                  pl.BlockSpec((None, 6, tn), lambda i, j, k: (i // per_batch, 0, j))],
        out_specs=pl.BlockSpec((tm, tn), lambda i, j, k: (i, j)),
        out_shape=jax.ShapeDtypeStruct((m, d), F32),
        compiler_params=_params("parallel", "parallel", "arbitrary"),
        name="ffn_down",
    )(a, w, x, mod)


def kernel(x, c, ada_w, ada_b, norm1_g, w_in, b_gates, lam_qk, attn_norm_g, sgu_ln_g, sgu_ln_b, sgu_w, sgu_b, conv_dw_w, conv_dw_b, conv_ln_g, conv_ln_b, w_proj_attn, w_proj_sgu, w_proj_conv, w_out, norm2_g, ffn_up, ffn_dw_w, ffn_dw_b, ffn_down, final_g):
    batch, seq, d = x.shape
    depth = ada_w.shape[0]
    assert (d, seq % 1024) == (D_MODEL, 0)
    m = batch * seq
    xf = x.reshape(m, d)

    c_pad = jnp.zeros((V7X_SUBLANES, d), F32).at[:batch].set(c)
    mod_all = _ada_mod(c_pad, ada_w, ada_b)

    half = HEAD_DIM // 2
    inv = ROPE_THETA ** (-jnp.arange(half, dtype=F32) / half)
    ang = jnp.arange(seq, dtype=F32)[:, None] * inv[None, :]
    cos_t = jnp.concatenate([jnp.cos(ang), jnp.cos(ang)], axis=-1)
    sin_t = jnp.concatenate([-jnp.sin(ang), jnp.sin(ang)], axis=-1)

    pad = FFN_PAD - FFN_DIM
    for l in range(depth):
        mod = mod_all[l, :batch].reshape(batch, 6, d)
        lam_init = 0.8 - 0.6 * math.exp(-0.3 * l)

        h = _norm_mod(xf, norm1_g[l], mod, seq, shift_row=0, scale_row=1)
        z = _matmul(h, _cast_weight(w_in, l, tr=256, tn=5632), tm=1024, tn=1024, out_dtype=BF16, name="in_proj")
        o_attn = _attention(z, cos_t, sin_t, lam_qk[l], attn_norm_g[l], batch, seq, lam_init)
        o_sgu = _sgu(z, sgu_ln_g[l], sgu_ln_b[l], sgu_w[l], sgu_b[l])
        o_conv = _conv_branch(z, conv_dw_w[l], conv_dw_b[l], conv_ln_g[l], conv_ln_b[l], seq)
        merged = _merge(o_attn, o_sgu, o_conv, _cast_weight(w_proj_attn, l, tr=256, tn=d),
                        _cast_weight(w_proj_sgu, l, tr=256, tn=d), _cast_weight(w_proj_conv, l, tr=256, tn=d),
                        z, b_gates[l].reshape(3, d))
        xf = _proj_residual(merged, _cast_weight(w_out, l, tr=256, tn=d), xf, mod, seq, gate_row=2)

        h2 = _norm_mod(xf, norm2_g[l], mod, seq, shift_row=3, scale_row=4)
        wa, wb = _cast_ffn_up(ffn_up, l)
        cw = jnp.pad(ffn_dw_w[l], ((0, 0), (0, pad)))
        cb = jnp.pad(ffn_dw_b[l], (0, pad)).reshape(1, FFN_PAD)
        act = _ffn_up(h2, wa, wb, cw, cb, seq)
        xf = _ffn_down(act, _cast_ffn_down(ffn_down, l), xf, mod, seq, gate_row=5)

    return _final_norm(xf, final_g).reshape(batch, seq, d)
```

```python
import functools
import math

import jax
import jax.numpy as jnp
from jax import lax
from jax.experimental import pallas as pl
from jax.experimental.pallas import tpu as pltpu

D_MODEL = 4096
HEAD_DIM = 128
ATTN_V_DIM = 2 * HEAD_DIM
ATTN_WIDTH = D_MODEL // 2
ATTN_HEADS = ATTN_WIDTH // ATTN_V_DIM
QK_WIDTH = ATTN_HEADS * 2 * HEAD_DIM
ROPE_THETA = 10000.0
SGU_WIDTH = D_MODEL // 4
SGU_CHUNK = 128
SGU_GROUPS = 8
CONV_WIDTH = D_MODEL // 4
CONV_KERNEL = 31
FFN_DIM = ((8 * D_MODEL // 3 + 255) // 256) * 256
FFN_CONV_KERNEL = 3
NORM_EPS = 1e-6
LN_EPS = 1e-5

COL_Q = 0
COL_K = QK_WIDTH
COL_V = 2 * QK_WIDTH
COL_SGU = COL_V + ATTN_WIDTH
COL_CONV = COL_SGU + 2 * SGU_WIDTH
COL_GATE = COL_CONV + 2 * CONV_WIDTH
IN_COLS = COL_GATE + 3 * D_MODEL

V7X_LANES = 128
V7X_SUBLANES = 8
V7X_VMEM_LIMIT_BYTES = 52 * 1024 * 1024
FFN_PAD = 11264
HALO = 32
NEG_BIG = -1e30

BF16 = jnp.bfloat16
F32 = jnp.float32


def _params(*sem):
    return pltpu.CompilerParams(dimension_semantics=sem, vmem_limit_bytes=V7X_VMEM_LIMIT_BYTES)


def _sigmoid(x):
    return 1.0 / (1.0 + jnp.exp(-x))


def _dot(a, b):
    return jnp.dot(a, b, preferred_element_type=F32)


def _cast_kernel(w_ref, o_ref):
    o_ref[...] = w_ref[...].astype(o_ref.dtype)


def _cast_weight(w_stack, l, *, tr, tn):
    _, r, n = w_stack.shape
    return pl.pallas_call(
        _cast_kernel,
        grid=(r // tr, n // tn),
        in_specs=[pl.BlockSpec((None, tr, tn), lambda i, j: (l, i, j))],
        out_specs=pl.BlockSpec((tr, tn), lambda i, j: (i, j)),
        out_shape=jax.ShapeDtypeStruct((r, n), BF16),
        compiler_params=_params("parallel", "parallel"),
        name="cast_weight",
    )(w_stack)


def _cast_ffn_down_kernel(w_ref, o_ref, *, n_real):
    i = pl.program_id(0)

    @pl.when(i < n_real)
    def _():
        o_ref[...] = w_ref[...].astype(o_ref.dtype)

    @pl.when(i >= n_real)
    def _():
        o_ref[...] = jnp.zeros(o_ref.shape, o_ref.dtype)


def _cast_ffn_down(ffn_down, l):
    _, _, d = ffn_down.shape
    tr = FFN_PAD - FFN_DIM
    n_real = FFN_DIM // tr
    return pl.pallas_call(
        functools.partial(_cast_ffn_down_kernel, n_real=n_real),
        grid=(FFN_PAD // tr,),
        in_specs=[pl.BlockSpec((None, tr, d), lambda i: (l, jnp.minimum(i, n_real - 1), 0))],
        out_specs=pl.BlockSpec((tr, d), lambda i: (i, 0)),
        out_shape=jax.ShapeDtypeStruct((FFN_PAD, d), BF16),
        compiler_params=_params("parallel"),
        name="cast_ffn_down",
    )(ffn_down)


def _ada_kernel(c_ref, w_ref, b_ref, o_ref):
    c = c_ref[...]
    c_act = c * _sigmoid(c)
    o_ref[...] = _dot(c_act.astype(BF16), w_ref[...].astype(BF16)) + b_ref[...]


def _ada_mod(c_pad, ada_w, ada_b):
    depth, d, n = ada_w.shape
    tn = 512
    rows = c_pad.shape[0]
    return pl.pallas_call(
        _ada_kernel,
        grid=(depth, n // tn),
        in_specs=[pl.BlockSpec((rows, d), lambda l, j: (0, 0)),
                  pl.BlockSpec((None, d, tn), lambda l, j: (l, 0, j)),
                  pl.BlockSpec((None, 1, tn), lambda l, j: (l, 0, j))],
        out_specs=pl.BlockSpec((None, rows, tn), lambda l, j: (l, 0, j)),
        out_shape=jax.ShapeDtypeStruct((depth, rows, n), F32),
        compiler_params=_params("parallel", "parallel"),
        name="ada_mod",
    )(c_pad, ada_w, ada_b.reshape(depth, 1, n))


def _norm_mod_kernel(x_ref, g_ref, mod_ref, o_ref, *, shift_row, scale_row):
    x = x_ref[...]
    y = x * lax.rsqrt(jnp.mean(x * x, axis=-1, keepdims=True) + NORM_EPS) * g_ref[...]
    sc = mod_ref[scale_row:scale_row + 1, :]
    sh = mod_ref[shift_row:shift_row + 1, :]
    o_ref[...] = (y * (1.0 + sc) + sh).astype(o_ref.dtype)


def _norm_mod(x, g, mod, seq, *, shift_row, scale_row):
    m, d = x.shape
    tm = 256
    per_batch = seq // tm
    return pl.pallas_call(
        functools.partial(_norm_mod_kernel, shift_row=shift_row, scale_row=scale_row),
        grid=(m // tm,),
        in_specs=[pl.BlockSpec((tm, d), lambda i: (i, 0)),
                  pl.BlockSpec((1, d), lambda i: (0, 0)),
                  pl.BlockSpec((None, 6, d), lambda i: (i // per_batch, 0, 0))],
        out_specs=pl.BlockSpec((tm, d), lambda i: (i, 0)),
        out_shape=jax.ShapeDtypeStruct((m, d), BF16),
        compiler_params=_params("parallel"),
        name="norm_mod",
    )(x, g.reshape(1, d), mod)


def _final_norm_kernel(x_ref, g_ref, o_ref):
    x = x_ref[...]
    o_ref[...] = x * lax.rsqrt(jnp.mean(x * x, axis=-1, keepdims=True) + NORM_EPS) * g_ref[...]


def _final_norm(x, g):
    m, d = x.shape
    tm = 256
    return pl.pallas_call(
        _final_norm_kernel,
        grid=(m // tm,),
        in_specs=[pl.BlockSpec((tm, d), lambda i: (i, 0)),
                  pl.BlockSpec((1, d), lambda i: (0, 0))],
        out_specs=pl.BlockSpec((tm, d), lambda i: (i, 0)),
        out_shape=jax.ShapeDtypeStruct((m, d), F32),
        compiler_params=_params("parallel"),
        name="final_norm",
    )(x, g.reshape(1, d))


def _in_proj_kernel(h_ref, w_ref, o_ref, w0, w1, *, ck):
    jj = pl.program_id(0)
    i = pl.program_id(1)
    rows = pl.ds(pl.multiple_of(i * ck, ck), ck)

    def fill(dst):
        dst[rows, :] = w_ref[...].astype(BF16)

    def compute(src):
        o_ref[...] = _dot(h_ref[...], src[...]).astype(o_ref.dtype)

    @pl.when(jj == 0)
    def _():
        fill(w0)

    @pl.when(jnp.logical_and(jj > 0, jj % 2 == 0))
    def _():
        fill(w0)
        compute(w1)

    @pl.when(jj % 2 == 1)
    def _():
        fill(w1)
        compute(w0)


def _in_proj(h, w_stack, l):
    m, k = h.shape
    n = w_stack.shape[2]
    tm, tn = 1024, 1024
    n_row_tiles, n_tiles = m // tm, n // tn
    ck = k // n_row_tiles
    return pl.pallas_call(
        functools.partial(_in_proj_kernel, ck=ck),
        grid=(n_tiles + 1, n_row_tiles),
        in_specs=[pl.BlockSpec((tm, k), lambda jj, i: (jnp.where(jj > 0, i, 0), 0)),
                  pl.BlockSpec((None, ck, tn), lambda jj, i: (l, i, jnp.minimum(jj, n_tiles - 1)))],
        out_specs=pl.BlockSpec((tm, tn), lambda jj, i: (jnp.where(jj > 0, i, 0), jnp.maximum(jj - 1, 0))),
        out_shape=jax.ShapeDtypeStruct((m, n), BF16),
        scratch_shapes=[pltpu.VMEM((k, tn), BF16), pltpu.VMEM((k, tn), BF16)],
        compiler_params=_params("arbitrary", "arbitrary"),
        name="in_proj",
    )(h, w_stack)


def _rotary(x, cos, sin):
    return x * cos + pltpu.roll(x, HEAD_DIM // 2, axis=1) * sin


def _attn_kernel(q_ref, k_ref, v_ref, cos_ref, sin_ref, lamqk_ref, g_ref, o_ref,
                 kt_ref, qd_ref, sa_ref, sb_ref, m_ref, l_ref, acc_ref, *, tq, tk, rc, lam_init):
    i = pl.program_id(2)
    seq = k_ref.shape[0]
    rows2 = 2 * tq
    n_lane_groups = tk // V7X_LANES

    @pl.when(i == 0)
    def _():
        def rope_rows(c, carry):
            r0 = pl.multiple_of(c * tk, tk)
            cos = cos_ref[pl.ds(r0, tk), :]
            sin = sin_ref[pl.ds(r0, tk), :]
            for g in range(2):
                x = k_ref[pl.ds(r0, tk), g * HEAD_DIM:(g + 1) * HEAD_DIM].astype(F32)
                kt_ref[c, g * HEAD_DIM:(g + 1) * HEAD_DIM, :] = _rotary(x, cos, sin).T.astype(BF16)
            return carry
        lax.fori_loop(0, seq // tk, rope_rows, 0)

    q0 = pl.multiple_of(i * tq, tq)
    cos_q = cos_ref[pl.ds(q0, tq), :]
    sin_q = sin_ref[pl.ds(q0, tq), :]
    q_scale = HEAD_DIM ** -0.5 * math.log2(math.e)
    zero = jnp.zeros((tq, HEAD_DIM), BF16)
    for g in range(2):
        x = q_ref[:, g * HEAD_DIM:(g + 1) * HEAD_DIM].astype(F32)
        qd_ref[g * tq:(g + 1) * tq, g * HEAD_DIM:(g + 1) * HEAD_DIM] = (_rotary(x, cos_q, sin_q) * q_scale).astype(BF16)
        qd_ref[g * tq:(g + 1) * tq, (1 - g) * HEAD_DIM:(2 - g) * HEAD_DIM] = zero

    m_ref[...] = jnp.full(m_ref.shape, NEG_BIG, F32)
    l_ref[...] = jnp.zeros(l_ref.shape, F32)
    acc_ref[...] = jnp.zeros(acc_ref.shape, F32)

    n_chunks = rows2 // rc
    n_parts = 4
    part_rows = rows2 // n_parts

    def scores_part(kb, s_ref, part):
        rows = slice(part * part_rows, (part + 1) * part_rows)
        s_ref[rows, :] = _dot(qd_ref[rows, :], kt_ref[kb])

    def update_chunk(kb, s_ref, c, masked):
        rows = slice(c * rc, (c + 1) * rc)
        sg = [s_ref[rows, g * V7X_LANES:(g + 1) * V7X_LANES] for g in range(n_lane_groups)]
        if masked:
            row = lax.broadcasted_iota(jnp.int32, (rc, V7X_LANES), 0) + (c * rc) % tq
            col = lax.broadcasted_iota(jnp.int32, (rc, V7X_LANES), 1)
            sg = [jnp.where(col + g * V7X_LANES <= row, x, NEG_BIG) for g, x in enumerate(sg)]
        mx = sg[0]
        for x in sg[1:]:
            mx = jnp.maximum(mx, x)
        m_prev = m_ref[rows, :]
        m_new = jnp.maximum(m_prev, jnp.max(mx, axis=-1, keepdims=True))
        alpha = jnp.exp2(m_prev - m_new)
        acc_ref[rows, :] = jnp.concatenate([alpha, alpha], axis=1) * acc_ref[rows, :]
        m_ref[rows, :] = m_new
        ps = [jnp.exp2(x - m_new) for x in sg]
        lsum = ps[0]
        for x in ps[1:]:
            lsum = lsum + x
        l_ref[rows, :] = alpha * l_ref[rows, :] + lsum
        p = jnp.concatenate([x.astype(BF16) for x in ps], axis=1)
        k0 = pl.multiple_of(kb * tk, tk)
        acc_ref[rows, :] += _dot(p, v_ref[pl.ds(k0, tk), :])

    def step(kb, s_ref, kb_next, s_next, masked):
        for c in range(n_chunks):
            if s_next is not None and c % (n_chunks // n_parts) == 0:
                scores_part(kb_next, s_next, c // (n_chunks // n_parts))
            update_chunk(kb, s_ref, c, masked)

    for part in range(n_parts):
        scores_part(0, sa_ref, part)

    def pair(t, carry):
        step(2 * t, sa_ref, 2 * t + 1, sb_ref, False)
        step(2 * t + 1, sb_ref, 2 * t + 2, sa_ref, False)
        return carry
    lax.fori_loop(0, i // 2, pair, 0)

    @pl.when(i % 2 == 0)
    def _():
        step(i, sa_ref, None, None, True)

    @pl.when(i % 2 == 1)
    def _():
        step(i - 1, sa_ref, i, sb_ref, False)
        step(i, sb_ref, None, None, True)

    lq = lamqk_ref[...]
    lam = (jnp.exp(jnp.sum(lq[0:1, :] * lq[1:2, :], axis=-1, keepdims=True))
           - jnp.exp(jnp.sum(lq[2:3, :] * lq[3:4, :], axis=-1, keepdims=True)) + lam_init)
    inv_l = 1.0 / jnp.sum(l_ref[...], axis=-1, keepdims=True)
    o = acc_ref[0:tq, :] * inv_l[0:tq] - lam * (acc_ref[tq:rows2, :] * inv_l[tq:rows2])
    y = o * lax.rsqrt(jnp.mean(o * o, axis=-1, keepdims=True) + NORM_EPS) * g_ref[...]
    o_ref[...] = (y * (1.0 - lam_init)).astype(o_ref.dtype)


def _attention(z, cos_t, sin_t, lam_qk, attn_norm_g, batch, seq, lam_init):
    m = z.shape[0]
    tq = tk = 512
    nq = seq // tq
    kcol0 = COL_K // ATTN_V_DIM
    vcol0 = COL_V // ATTN_V_DIM
    return pl.pallas_call(
        functools.partial(_attn_kernel, tq=tq, tk=tk, rc=128, lam_init=lam_init),
        grid=(batch, ATTN_HEADS, nq),
        in_specs=[pl.BlockSpec((tq, ATTN_V_DIM), lambda b, h, i: (b * nq + i, h)),
                  pl.BlockSpec((seq, ATTN_V_DIM), lambda b, h, i: (b, kcol0 + h)),
                  pl.BlockSpec((seq, ATTN_V_DIM), lambda b, h, i: (b, vcol0 + h)),
                  pl.BlockSpec((seq, HEAD_DIM), lambda b, h, i: (0, 0)),
                  pl.BlockSpec((seq, HEAD_DIM), lambda b, h, i: (0, 0)),
                  pl.BlockSpec((4, HEAD_DIM), lambda b, h, i: (0, 0)),
                  pl.BlockSpec((1, ATTN_V_DIM), lambda b, h, i: (0, 0))],
        out_specs=pl.BlockSpec((tq, ATTN_V_DIM), lambda b, h, i: (b * nq + i, h)),
        out_shape=jax.ShapeDtypeStruct((m, ATTN_WIDTH), BF16),
        scratch_shapes=[pltpu.VMEM((seq // tk, ATTN_V_DIM, tk), BF16),
                        pltpu.VMEM((2 * tq, ATTN_V_DIM), BF16),
                        pltpu.VMEM((2 * tq, tk), F32),
                        pltpu.VMEM((2 * tq, tk), F32),
                        pltpu.VMEM((2 * tq, V7X_LANES), F32),
                        pltpu.VMEM((2 * tq, V7X_LANES), F32),
                        pltpu.VMEM((2 * tq, ATTN_V_DIM), F32)],
        compiler_params=_params("arbitrary", "arbitrary", "arbitrary"),
        name="diff_attn",
    )(z, z, z, cos_t, sin_t, lam_qk, attn_norm_g.reshape(1, ATTN_V_DIM))


def _gelu(x):
    return 0.5 * x * (1.0 + lax.erf(x * (2.0 ** -0.5)))


def _sgu_kernel(zu_ref, zv_ref, g_ref, b_ref, w_ref, bt_ref, o_ref, *, tm):
    sv = _gelu(zv_ref[...].astype(F32))
    mu = jnp.mean(sv, axis=-1, keepdims=True)
    var = jnp.mean(jnp.square(sv - mu), axis=-1, keepdims=True)
    sv = ((sv - mu) * lax.rsqrt(var + LN_EPS) * g_ref[...] + b_ref[...]).astype(BF16)
    row = lax.broadcasted_iota(jnp.int32, (SGU_CHUNK, SGU_CHUNK), 0)
    col = lax.broadcasted_iota(jnp.int32, (SGU_CHUNK, SGU_CHUNK), 1)
    bt = bt_ref[...]
    for g in range(SGU_GROUPS):
        cols = slice(g * V7X_LANES, (g + 1) * V7X_LANES)
        ws = jnp.where(col <= row, w_ref[g], 0.0).astype(BF16)
        bias = bt[:, g:g + 1]
        for ch in range(tm // SGU_CHUNK):
            rows = slice(ch * SGU_CHUNK, (ch + 1) * SGU_CHUNK)
            sp = _dot(ws, sv[rows, cols]) + bias
            u = _gelu(zu_ref[rows, cols].astype(F32))
            o_ref[rows, cols] = (u * sp).astype(o_ref.dtype)


def _sgu(z, ln_g, ln_b, sgu_w, sgu_b):
    m = z.shape[0]
    tm = 512
    ucol = COL_SGU // SGU_WIDTH
    return pl.pallas_call(
        functools.partial(_sgu_kernel, tm=tm),
        grid=(m // tm,),
        in_specs=[pl.BlockSpec((tm, SGU_WIDTH), lambda i: (i, ucol)),
                  pl.BlockSpec((tm, SGU_WIDTH), lambda i: (i, ucol + 1)),
                  pl.BlockSpec((1, SGU_WIDTH), lambda i: (0, 0)),
                  pl.BlockSpec((1, SGU_WIDTH), lambda i: (0, 0)),
                  pl.BlockSpec((SGU_GROUPS, SGU_CHUNK, SGU_CHUNK), lambda i: (0, 0, 0)),
                  pl.BlockSpec((SGU_CHUNK, SGU_GROUPS), lambda i: (0, 0))],
        out_specs=pl.BlockSpec((tm, SGU_WIDTH), lambda i: (i, 0)),
        out_shape=jax.ShapeDtypeStruct((m, SGU_WIDTH), BF16),
        compiler_params=_params("parallel"),
        name="sgu",
    )(z, z, ln_g.reshape(1, -1), ln_b.reshape(1, -1), sgu_w, sgu_b.T)


def _conv_kernel(za_ref, zb_ref, w_ref, b_ref, g_ref, beta_ref, o_ref, ybuf, yconv, *, tm, per_batch):
    i = pl.program_id(0)

    @pl.when(i % per_batch == 0)
    def _():
        ybuf[0:HALO, :] = jnp.zeros((HALO, CONV_WIDTH), F32)

    ybuf[HALO:HALO + tm, :] = za_ref[...].astype(F32) * _sigmoid(zb_ref[...].astype(F32))

    rc = 128
    for c in range(CONV_WIDTH // V7X_LANES):
        cols = slice(c * V7X_LANES, (c + 1) * V7X_LANES)
        w = w_ref[:, cols]
        for r in range(tm // rc):
            base = HALO + r * rc - V7X_SUBLANES
            acc = jnp.broadcast_to(b_ref[:, cols], (rc, V7X_LANES))
            for b in range(V7X_SUBLANES):
                u = None
                for k in range(b, CONV_KERNEL, V7X_SUBLANES):
                    tap = CONV_KERNEL - 1 - k
                    term = w[tap:tap + 1, :] * ybuf[base - (k - b):base - (k - b) + rc + V7X_SUBLANES, cols]
                    u = term if u is None else u + term
                acc = acc + u[V7X_SUBLANES - b:V7X_SUBLANES - b + rc, :]
            yconv[r * rc:(r + 1) * rc, cols] = acc

    ybuf[0:HALO, :] = ybuf[tm:tm + HALO, :]

    y = yconv[...]
    mu = jnp.mean(y, axis=-1, keepdims=True)
    var = jnp.mean(jnp.square(y - mu), axis=-1, keepdims=True)
    yn = (y - mu) * lax.rsqrt(var + LN_EPS) * g_ref[...] + beta_ref[...]
    o_ref[...] = (yn * _sigmoid(yn)).astype(o_ref.dtype)


def _conv_branch(z, w, b, ln_g, ln_b, seq):
    m = z.shape[0]
    tm = 512
    acol = COL_CONV // CONV_WIDTH
    return pl.pallas_call(
        functools.partial(_conv_kernel, tm=tm, per_batch=seq // tm),
        grid=(m // tm,),
        in_specs=[pl.BlockSpec((tm, CONV_WIDTH), lambda i: (i, acol)),
                  pl.BlockSpec((tm, CONV_WIDTH), lambda i: (i, acol + 1)),
                  pl.BlockSpec((CONV_KERNEL, CONV_WIDTH), lambda i: (0, 0)),
                  pl.BlockSpec((1, CONV_WIDTH), lambda i: (0, 0)),
                  pl.BlockSpec((1, CONV_WIDTH), lambda i: (0, 0)),
                  pl.BlockSpec((1, CONV_WIDTH), lambda i: (0, 0))],
        out_specs=pl.BlockSpec((tm, CONV_WIDTH), lambda i: (i, 0)),
        out_shape=jax.ShapeDtypeStruct((m, CONV_WIDTH), BF16),
        scratch_shapes=[pltpu.VMEM((HALO + tm, CONV_WIDTH), F32),
                        pltpu.VMEM((tm, CONV_WIDTH), F32)],
        compiler_params=_params("arbitrary"),
        name="conv_branch",
    )(z, z, w, b.reshape(1, -1), ln_g.reshape(1, -1), ln_b.reshape(1, -1))


def _merge_kernel(oa_ref, os_ref, oc_ref, wa_ref, ws_ref, wc_ref,
                  ga_ref, gs_ref, gc_ref, bg_ref, o_ref):
    def gate(z_ref, idx):
        return _sigmoid(z_ref[...].astype(F32) + bg_ref[idx:idx + 1, :])

    acc = gate(ga_ref, 0) * _dot(oa_ref[...], wa_ref[...])
    acc = acc + gate(gs_ref, 1) * _dot(os_ref[...], ws_ref[...])
    acc = acc + gate(gc_ref, 2) * _dot(oc_ref[...], wc_ref[...])
    o_ref[...] = acc.astype(o_ref.dtype)


def _merge(o_attn, o_sgu, o_conv, wpa, wps, wpc, z, b_gates3):
    m = z.shape[0]
    d = D_MODEL
    tm, tn = 512, 1024
    gcol0 = COL_GATE // tn
    nb = d // tn
    return pl.pallas_call(
        _merge_kernel,
        grid=(d // tn, m // tm),
        in_specs=[pl.BlockSpec((tm, ATTN_WIDTH), lambda j, i: (i, 0)),
                  pl.BlockSpec((tm, SGU_WIDTH), lambda j, i: (i, 0)),
                  pl.BlockSpec((tm, CONV_WIDTH), lambda j, i: (i, 0)),
                  pl.BlockSpec((ATTN_WIDTH, tn), lambda j, i: (0, j)),
                  pl.BlockSpec((SGU_WIDTH, tn), lambda j, i: (0, j)),
                  pl.BlockSpec((CONV_WIDTH, tn), lambda j, i: (0, j)),
                  pl.BlockSpec((tm, tn), lambda j, i: (i, gcol0 + j)),
                  pl.BlockSpec((tm, tn), lambda j, i: (i, gcol0 + nb + j)),
                  pl.BlockSpec((tm, tn), lambda j, i: (i, gcol0 + 2 * nb + j)),
                  pl.BlockSpec((3, tn), lambda j, i: (0, j))],
        out_specs=pl.BlockSpec((tm, tn), lambda j, i: (i, j)),
        out_shape=jax.ShapeDtypeStruct((m, d), BF16),
        compiler_params=_params("parallel", "parallel"),
        name="merge",
    )(o_attn, o_sgu, o_conv, wpa, wps, wpc, z, z, z, b_gates3)


def _proj_res_kernel(a_ref, w_ref, x_ref, mod_ref, o_ref, *, gate_row):
    gate = mod_ref[gate_row:gate_row + 1, :]
    o_ref[...] = x_ref[...] + gate * _dot(a_ref[...], w_ref[...])


def _proj_residual(a, w, x, mod, seq, *, gate_row):
    m, k = a.shape
    d = w.shape[1]
    tm, tn = 1024, 512
    per_batch = seq // tm
    return pl.pallas_call(
        functools.partial(_proj_res_kernel, gate_row=gate_row),
        grid=(m // tm, d // tn),
        in_specs=[pl.BlockSpec((tm, k), lambda i, j: (i, 0)),
                  pl.BlockSpec((k, tn), lambda i, j: (0, j)),
                  pl.BlockSpec((tm, tn), lambda i, j: (i, j)),
                  pl.BlockSpec((None, 6, tn), lambda i, j: (i // per_batch, 0, j))],
        out_specs=pl.BlockSpec((tm, tn), lambda i, j: (i, j)),
        out_shape=jax.ShapeDtypeStruct((m, d), F32),
        compiler_params=_params("parallel", "parallel"),
        name="proj_residual",
    )(a, w, x, mod)


def _ffn_up_kernel(h_ref, a0_ref, a1_ref, b0_ref, b1_ref, cw_ref, cb_ref, o_ref, w0, w1, fbuf,
                   *, tm, tn, ck, n_tiles, per_batch):
    jj = pl.program_id(0)
    i = pl.program_id(1)
    rows = pl.ds(pl.multiple_of(i * ck, ck), ck)
    half = tn // 2
    past_end = jj >= n_tiles - 1

    @pl.when(i % per_batch == 0)
    def _():
        fbuf[0:V7X_SUBLANES, :] = jnp.zeros((V7X_SUBLANES, tn), F32)

    def fill(dst):
        dst[rows, 0:half] = a0_ref[...].astype(BF16)
        dst[rows, half:tn] = jnp.where(past_end, 0.0, a1_ref[...]).astype(BF16)
        dst[rows, tn:tn + half] = b0_ref[...].astype(BF16)
        dst[rows, tn + half:2 * tn] = jnp.where(past_end, 0.0, b1_ref[...]).astype(BF16)

    def compute(src):
        fab = _dot(h_ref[...], src[...])
        fbuf[V7X_SUBLANES:V7X_SUBLANES + tm, :] = fab[:, 0:tn]
        cw = cw_ref[...]
        y = cb_ref[...] + cw[2:3, :] * fbuf[V7X_SUBLANES:V7X_SUBLANES + tm, :]
        y = y + cw[1:2, :] * fbuf[V7X_SUBLANES - 1:V7X_SUBLANES - 1 + tm, :]
        y = y + cw[0:1, :] * fbuf[V7X_SUBLANES - 2:V7X_SUBLANES - 2 + tm, :]
        fbuf[0:V7X_SUBLANES, :] = fbuf[tm:tm + V7X_SUBLANES, :]
        o_ref[...] = (y * _sigmoid(y) * fab[:, tn:2 * tn]).astype(o_ref.dtype)

    @pl.when(jj == 0)
    def _():
        fill(w0)

    @pl.when(jnp.logical_and(jj > 0, jj % 2 == 0))
    def _():
        fill(w0)
        compute(w1)

    @pl.when(jj % 2 == 1)
    def _():
        fill(w1)
        compute(w0)


def _ffn_up(h, ffn_up, l, cw, cb, seq):
    m, d = h.shape
    tm, tn = 1024, 512
    half = tn // 2
    n_row_tiles, n_tiles = m // tm, FFN_PAD // tn
    ck = d // n_row_tiles
    n_half_blocks = FFN_DIM // half
    last_block = 2 * n_half_blocks - 1

    def wspec(offset):
        def index(jj, i):
            t = jnp.minimum(jj, n_tiles - 1)
            return (l, i, jnp.minimum(offset + 2 * t, last_block))
        return pl.BlockSpec((None, ck, half), index)

    def row_tile(jj, i):
        return jnp.where(jj > 0, i, 0)

    def col_tile(jj, i):
        return jnp.maximum(jj - 1, 0)

    return pl.pallas_call(
        functools.partial(_ffn_up_kernel, tm=tm, tn=tn, ck=ck, n_tiles=n_tiles, per_batch=seq // tm),
        grid=(n_tiles + 1, n_row_tiles),
        in_specs=[pl.BlockSpec((tm, d), lambda jj, i: (row_tile(jj, i), 0)),
                  wspec(0), wspec(1), wspec(n_half_blocks), wspec(n_half_blocks + 1),
                  pl.BlockSpec((FFN_CONV_KERNEL, tn), lambda jj, i: (0, col_tile(jj, i))),
                  pl.BlockSpec((1, tn), lambda jj, i: (0, col_tile(jj, i)))],
        out_specs=pl.BlockSpec((tm, tn), lambda jj, i: (row_tile(jj, i), col_tile(jj, i))),
        out_shape=jax.ShapeDtypeStruct((m, FFN_PAD), BF16),
        scratch_shapes=[pltpu.VMEM((d, 2 * tn), BF16), pltpu.VMEM((d, 2 * tn), BF16),
                        pltpu.VMEM((V7X_SUBLANES + tm, tn), F32)],
        compiler_params=_params("arbitrary", "arbitrary"),
        name="ffn_up",
    )(h, ffn_up, ffn_up, ffn_up, ffn_up, cw, cb)


def _ffn_down_kernel(a_ref, w_ref, x_ref, mod_ref, o_ref, *, gate_row, nk):
    k = pl.program_id(2)

    @pl.when(k == 0)
    def _():
        o_ref[...] = _dot(a_ref[...], w_ref[...])

    @pl.when(jnp.logical_and(k > 0, k < nk - 1))
    def _():
        o_ref[...] += _dot(a_ref[...], w_ref[...])

    @pl.when(k == nk - 1)
    def _():
        gate = mod_ref[gate_row:gate_row + 1, :]
        o_ref[...] = x_ref[...] + gate * (o_ref[...] + _dot(a_ref[...], w_ref[...]))


def _ffn_down(a, w, x, mod, seq, *, gate_row):
    m, kdim = a.shape
    d = w.shape[1]
    tm, tn, tk = 1024, 1024, 2816
    nk = kdim // tk
    per_batch = seq // tm
    return pl.pallas_call(
        functools.partial(_ffn_down_kernel, gate_row=gate_row, nk=nk),
        grid=(m // tm, d // tn, nk),
        in_specs=[pl.BlockSpec((tm, tk), lambda i, j, k: (i, k)),
                  pl.BlockSpec((tk, tn), lambda i, j, k: (k, j)),
                  pl.BlockSpec((tm, tn), lambda i, j, k: (i, j)),
                  pl.BlockSpec((None, 6, tn), lambda i, j, k: (i // per_batch, 0, j))],
        out_specs=pl.BlockSpec((tm, tn), lambda i, j, k: (i, j)),
        out_shape=jax.ShapeDtypeStruct((m, d), F32),
        compiler_params=_params("parallel", "parallel", "arbitrary"),
        name="ffn_down",
    )(a, w, x, mod)


def kernel(x, c, ada_w, ada_b, norm1_g, w_in, b_gates, lam_qk, attn_norm_g, sgu_ln_g, sgu_ln_b, sgu_w, sgu_b, conv_dw_w, conv_dw_b, conv_ln_g, conv_ln_b, w_proj_attn, w_proj_sgu, w_proj_conv, w_out, norm2_g, ffn_up, ffn_dw_w, ffn_dw_b, ffn_down, final_g):
    batch, seq, d = x.shape
    depth = ada_w.shape[0]
    assert (d, seq % 1024) == (D_MODEL, 0)
    m = batch * seq
    xf = x.reshape(m, d)

    c_pad = jnp.zeros((V7X_SUBLANES, d), F32).at[:batch].set(c)
    mod_all = _ada_mod(c_pad, ada_w, ada_b)

    half = HEAD_DIM // 2
    inv = ROPE_THETA ** (-jnp.arange(half, dtype=F32) / half)
    ang = jnp.arange(seq, dtype=F32)[:, None] * inv[None, :]
    cos_t = jnp.concatenate([jnp.cos(ang), jnp.cos(ang)], axis=-1)
    sin_t = jnp.concatenate([-jnp.sin(ang), jnp.sin(ang)], axis=-1)

    pad = FFN_PAD - FFN_DIM
    for l in range(depth):
        mod = mod_all[l, :batch].reshape(batch, 6, d)
        lam_init = 0.8 - 0.6 * math.exp(-0.3 * l)

        h = _norm_mod(xf, norm1_g[l], mod, seq, shift_row=0, scale_row=1)
        z = _in_proj(h, w_in, l)
        o_attn = _attention(z, cos_t, sin_t, lam_qk[l], attn_norm_g[l], batch, seq, lam_init)
        o_sgu = _sgu(z, sgu_ln_g[l], sgu_ln_b[l], sgu_w[l], sgu_b[l])
        o_conv = _conv_branch(z, conv_dw_w[l], conv_dw_b[l], conv_ln_g[l], conv_ln_b[l], seq)
        merged = _merge(o_attn, o_sgu, o_conv, _cast_weight(w_proj_attn, l, tr=256, tn=d),
                        _cast_weight(w_proj_sgu, l, tr=256, tn=d), _cast_weight(w_proj_conv, l, tr=256, tn=d),
                        z, b_gates[l].reshape(3, d))
        xf = _proj_residual(merged, _cast_weight(w_out, l, tr=256, tn=d), xf, mod, seq, gate_row=2)

        h2 = _norm_mod(xf, norm2_g[l], mod, seq, shift_row=3, scale_row=4)
        cw = jnp.pad(ffn_dw_w[l], ((0, 0), (0, pad)))
        cb = jnp.pad(ffn_dw_b[l], (0, pad)).reshape(1, FFN_PAD)
        act = _ffn_up(h2, ffn_up, l, cw, cb, seq)
        xf = _ffn_down(act, _cast_ffn_down(ffn_down, l), xf, mod, seq, gate_row=5)

    return _final_norm(xf, final_g).reshape(batch, seq, d)
```

```python
import functools
import math

import jax
import jax.numpy as jnp
from jax import lax
from jax.experimental import pallas as pl
from jax.experimental.pallas import tpu as pltpu

D_MODEL = 4096
HEAD_DIM = 128
ATTN_V_DIM = 2 * HEAD_DIM
ATTN_WIDTH = D_MODEL // 2
ATTN_HEADS = ATTN_WIDTH // ATTN_V_DIM
QK_WIDTH = ATTN_HEADS * 2 * HEAD_DIM
ROPE_THETA = 10000.0
SGU_WIDTH = D_MODEL // 4
SGU_CHUNK = 128
SGU_GROUPS = 8
CONV_WIDTH = D_MODEL // 4
CONV_KERNEL = 31
FFN_DIM = ((8 * D_MODEL // 3 + 255) // 256) * 256
FFN_CONV_KERNEL = 3
NORM_EPS = 1e-6
LN_EPS = 1e-5

COL_Q = 0
COL_K = QK_WIDTH
COL_V = 2 * QK_WIDTH
COL_SGU = COL_V + ATTN_WIDTH
COL_CONV = COL_SGU + 2 * SGU_WIDTH
COL_GATE = COL_CONV + 2 * CONV_WIDTH
IN_COLS = COL_GATE + 3 * D_MODEL

V7X_LANES = 128
V7X_SUBLANES = 8
V7X_VMEM_LIMIT_BYTES = 52 * 1024 * 1024
FFN_PAD = 11264
HALO = 32
NEG_BIG = -1e30

BF16 = jnp.bfloat16
F32 = jnp.float32


def _params(*sem):
    return pltpu.CompilerParams(dimension_semantics=sem, vmem_limit_bytes=V7X_VMEM_LIMIT_BYTES)


def _sigmoid(x):
    return 1.0 / (1.0 + jnp.exp(-x))


def _dot(a, b):
    return jnp.dot(a, b, preferred_element_type=F32)


def _cast_kernel(w_ref, o_ref):
    o_ref[...] = w_ref[...].astype(o_ref.dtype)


def _cast_weight(w_stack, l, *, tr, tn):
    _, r, n = w_stack.shape
    return pl.pallas_call(
        _cast_kernel,
        grid=(r // tr, n // tn),
        in_specs=[pl.BlockSpec((None, tr, tn), lambda i, j: (l, i, j))],
        out_specs=pl.BlockSpec((tr, tn), lambda i, j: (i, j)),
        out_shape=jax.ShapeDtypeStruct((r, n), BF16),
        compiler_params=_params("parallel", "parallel"),
        name="cast_weight",
    )(w_stack)


def _ada_kernel(c_ref, w_ref, b_ref, o_ref):
    c = c_ref[...]
    c_act = c * _sigmoid(c)
    o_ref[...] = _dot(c_act.astype(BF16), w_ref[...].astype(BF16)) + b_ref[...]


def _ada_mod(c_pad, ada_w, ada_b):
    depth, d, n = ada_w.shape
    tn = 512
    rows = c_pad.shape[0]
    return pl.pallas_call(
        _ada_kernel,
        grid=(depth, n // tn),
        in_specs=[pl.BlockSpec((rows, d), lambda l, j: (0, 0)),
                  pl.BlockSpec((None, d, tn), lambda l, j: (l, 0, j)),
                  pl.BlockSpec((None, 1, tn), lambda l, j: (l, 0, j))],
        out_specs=pl.BlockSpec((None, rows, tn), lambda l, j: (l, 0, j)),
        out_shape=jax.ShapeDtypeStruct((depth, rows, n), F32),
        compiler_params=_params("parallel", "parallel"),
        name="ada_mod",
    )(c_pad, ada_w, ada_b.reshape(depth, 1, n))


def _norm_mod_kernel(x_ref, g_ref, mod_ref, o_ref, *, shift_row, scale_row):
    strip = 2 * V7X_SUBLANES
    for r in range(0, x_ref.shape[0], strip):
        rows = slice(r, r + strip)
        x = x_ref[rows, :]
        y = x * lax.rsqrt(jnp.mean(x * x, axis=-1, keepdims=True) + NORM_EPS) * g_ref[...]
        sc = mod_ref[scale_row:scale_row + 1, :]
        sh = mod_ref[shift_row:shift_row + 1, :]
        o_ref[rows, :] = (y * (1.0 + sc) + sh).astype(o_ref.dtype)


def _norm_mod(x, g, mod, seq, *, shift_row, scale_row):
    m, d = x.shape
    tm = 512
    per_batch = seq // tm
    return pl.pallas_call(
        functools.partial(_norm_mod_kernel, shift_row=shift_row, scale_row=scale_row),
        grid=(m // tm,),
        in_specs=[pl.BlockSpec((tm, d), lambda i: (i, 0)),
                  pl.BlockSpec((1, d), lambda i: (0, 0)),
                  pl.BlockSpec((None, 6, d), lambda i: (i // per_batch, 0, 0))],
        out_specs=pl.BlockSpec((tm, d), lambda i: (i, 0)),
        out_shape=jax.ShapeDtypeStruct((m, d), BF16),
        compiler_params=_params("parallel"),
        name="norm_mod",
    )(x, g.reshape(1, d), mod)


def _final_norm_kernel(x_ref, g_ref, o_ref):
    strip = 2 * V7X_SUBLANES
    for r in range(0, x_ref.shape[0], strip):
        rows = slice(r, r + strip)
        x = x_ref[rows, :]
        o_ref[rows, :] = x * lax.rsqrt(jnp.mean(x * x, axis=-1, keepdims=True) + NORM_EPS) * g_ref[...]


def _final_norm(x, g):
    m, d = x.shape
    tm = 512
    return pl.pallas_call(
        _final_norm_kernel,
        grid=(m // tm,),
        in_specs=[pl.BlockSpec((tm, d), lambda i: (i, 0)),
                  pl.BlockSpec((1, d), lambda i: (0, 0))],
        out_specs=pl.BlockSpec((tm, d), lambda i: (i, 0)),
        out_shape=jax.ShapeDtypeStruct((m, d), F32),
        compiler_params=_params("parallel"),
        name="final_norm",
    )(x, g.reshape(1, d))


def _in_proj_kernel(h_ref, w_ref, sa_ref, sb_ref, o_ref, osa_ref, osb_ref, w0, w1, *, ck):
    jj = pl.program_id(0)
    i = pl.program_id(1)
    rows = pl.ds(pl.multiple_of(i * ck, ck), ck)

    def fill(dst):
        dst[rows, :] = w_ref[...].astype(BF16)
        osa_ref[...] = sa_ref[...].astype(BF16)
        osb_ref[...] = sb_ref[...].astype(BF16)

    def compute(src):
        o_ref[...] = _dot(h_ref[...], src[...]).astype(o_ref.dtype)

    @pl.when(jj == 0)
    def _():
        fill(w0)

    @pl.when(jnp.logical_and(jj > 0, jj % 2 == 0))
    def _():
        fill(w0)
        compute(w1)

    @pl.when(jj % 2 == 1)
    def _():
        fill(w1)
        compute(w0)


def _side_block_rows(row_counts, n_steps):
    br = 2 * V7X_SUBLANES
    while sum(r // br for r in row_counts) > n_steps:
        br *= 2
    assert all(r % br == 0 for r in row_counts)
    return br


def _in_proj(h, w_stack, l, side_a, side_b):
    m, k = h.shape
    n = w_stack.shape[2]
    tm, tn = 1024, 1024
    n_row_tiles, n_tiles = m // tm, n // tn
    ck = k // n_row_tiles
    ra, rb = side_a.shape[1], side_b.shape[1]
    da, db = side_a.shape[2], side_b.shape[2]
    br = _side_block_rows((ra, rb), (n_tiles + 1) * n_row_tiles)
    na, nb = ra // br, rb // br

    def a_block(jj, i):
        return jnp.minimum(jj * n_row_tiles + i, na - 1)

    def b_block(jj, i):
        return jnp.clip(jj * n_row_tiles + i - na, 0, nb - 1)

    return pl.pallas_call(
        functools.partial(_in_proj_kernel, ck=ck),
        grid=(n_tiles + 1, n_row_tiles),
        in_specs=[pl.BlockSpec((tm, k), lambda jj, i: (jnp.where(jj > 0, i, 0), 0)),
                  pl.BlockSpec((None, ck, tn), lambda jj, i: (l, i, jnp.minimum(jj, n_tiles - 1))),
                  pl.BlockSpec((None, br, da), lambda jj, i: (l, a_block(jj, i), 0)),
                  pl.BlockSpec((None, br, db), lambda jj, i: (l, b_block(jj, i), 0))],
        out_specs=[pl.BlockSpec((tm, tn), lambda jj, i: (jnp.where(jj > 0, i, 0), jnp.maximum(jj - 1, 0))),
                   pl.BlockSpec((br, da), lambda jj, i: (a_block(jj, i), 0)),
                   pl.BlockSpec((br, db), lambda jj, i: (b_block(jj, i), 0))],
        out_shape=[jax.ShapeDtypeStruct((m, n), BF16),
                   jax.ShapeDtypeStruct((ra, da), BF16),
                   jax.ShapeDtypeStruct((rb, db), BF16)],
        scratch_shapes=[pltpu.VMEM((k, tn), BF16), pltpu.VMEM((k, tn), BF16)],
        compiler_params=_params("arbitrary", "arbitrary"),
        name="in_proj",
    )(h, w_stack, side_a, side_b)


def _rotary(x, cos, sin):
    return x * cos + pltpu.roll(x, HEAD_DIM // 2, axis=1) * sin


def _attn_kernel(q_ref, k_ref, v_ref, cos_ref, sin_ref, lamqk_ref, g_ref, o_ref,
                 kt_ref, qd_ref, sa_ref, sb_ref, m_ref, l_ref, acc_ref, *, tq, tk, rc, lam_init):
    i = pl.program_id(2)
    seq = k_ref.shape[0]
    rows2 = 2 * tq
    n_lane_groups = tk // V7X_LANES

    @pl.when(i == 0)
    def _():
        def rope_rows(c, carry):
            r0 = pl.multiple_of(c * tk, tk)
            cos = cos_ref[pl.ds(r0, tk), :]
            sin = sin_ref[pl.ds(r0, tk), :]
            for g in range(2):
                x = k_ref[pl.ds(r0, tk), g * HEAD_DIM:(g + 1) * HEAD_DIM].astype(F32)
                kt_ref[c, g * HEAD_DIM:(g + 1) * HEAD_DIM, :] = _rotary(x, cos, sin).T.astype(BF16)
            return carry
        lax.fori_loop(0, seq // tk, rope_rows, 0)

    q0 = pl.multiple_of(i * tq, tq)
    cos_q = cos_ref[pl.ds(q0, tq), :]
    sin_q = sin_ref[pl.ds(q0, tq), :]
    q_scale = HEAD_DIM ** -0.5 * math.log2(math.e)
    zero = jnp.zeros((tq, HEAD_DIM), BF16)
    for g in range(2):
        x = q_ref[:, g * HEAD_DIM:(g + 1) * HEAD_DIM].astype(F32)
        qd_ref[g * tq:(g + 1) * tq, g * HEAD_DIM:(g + 1) * HEAD_DIM] = (_rotary(x, cos_q, sin_q) * q_scale).astype(BF16)
        qd_ref[g * tq:(g + 1) * tq, (1 - g) * HEAD_DIM:(2 - g) * HEAD_DIM] = zero

    m_ref[...] = jnp.full(m_ref.shape, NEG_BIG, F32)
    l_ref[...] = jnp.zeros(l_ref.shape, F32)
    acc_ref[...] = jnp.zeros(acc_ref.shape, F32)

    n_chunks = rows2 // rc
    n_parts = 4
    part_rows = rows2 // n_parts

    def scores_part(kb, s_ref, part):
        rows = slice(part * part_rows, (part + 1) * part_rows)
        s_ref[rows, :] = _dot(qd_ref[rows, :], kt_ref[kb])

    def update_chunk(kb, s_ref, c, masked):
        rows = slice(c * rc, (c + 1) * rc)
        sg = [s_ref[rows, g * V7X_LANES:(g + 1) * V7X_LANES] for g in range(n_lane_groups)]
        if masked:
            row = lax.broadcasted_iota(jnp.int32, (rc, V7X_LANES), 0) + (c * rc) % tq
            col = lax.broadcasted_iota(jnp.int32, (rc, V7X_LANES), 1)
            sg = [jnp.where(col + g * V7X_LANES <= row, x, NEG_BIG) for g, x in enumerate(sg)]
        mx = sg[0]
        for x in sg[1:]:
            mx = jnp.maximum(mx, x)
        m_prev = m_ref[rows, :]
        m_new = jnp.maximum(m_prev, jnp.max(mx, axis=-1, keepdims=True))
        alpha = jnp.exp2(m_prev - m_new)
        acc_ref[rows, :] = jnp.concatenate([alpha, alpha], axis=1) * acc_ref[rows, :]
        m_ref[rows, :] = m_new
        ps = [jnp.exp2(x - m_new) for x in sg]
        lsum = ps[0]
        for x in ps[1:]:
            lsum = lsum + x
        l_ref[rows, :] = alpha * l_ref[rows, :] + lsum
        p = jnp.concatenate([x.astype(BF16) for x in ps], axis=1)
        k0 = pl.multiple_of(kb * tk, tk)
        acc_ref[rows, :] += _dot(p, v_ref[pl.ds(k0, tk), :])

    def step(kb, s_ref, kb_next, s_next, masked):
        for c in range(n_chunks):
            if s_next is not None and c % (n_chunks // n_parts) == 0:
                scores_part(kb_next, s_next, c // (n_chunks // n_parts))
            update_chunk(kb, s_ref, c, masked)

    for part in range(n_parts):
        scores_part(0, sa_ref, part)

    def pair(t, carry):
        step(2 * t, sa_ref, 2 * t + 1, sb_ref, False)
        step(2 * t + 1, sb_ref, 2 * t + 2, sa_ref, False)
        return carry
    lax.fori_loop(0, i // 2, pair, 0)

    @pl.when(i % 2 == 0)
    def _():
        step(i, sa_ref, None, None, True)

    @pl.when(i % 2 == 1)
    def _():
        step(i - 1, sa_ref, i, sb_ref, False)
        step(i, sb_ref, None, None, True)

    lq = lamqk_ref[...]
    lam = (jnp.exp(jnp.sum(lq[0:1, :] * lq[1:2, :], axis=-1, keepdims=True))
           - jnp.exp(jnp.sum(lq[2:3, :] * lq[3:4, :], axis=-1, keepdims=True)) + lam_init)
    inv_l = 1.0 / jnp.sum(l_ref[...], axis=-1, keepdims=True)
    o = acc_ref[0:tq, :] * inv_l[0:tq] - lam * (acc_ref[tq:rows2, :] * inv_l[tq:rows2])
    y = o * lax.rsqrt(jnp.mean(o * o, axis=-1, keepdims=True) + NORM_EPS) * g_ref[...]
    o_ref[...] = (y * (1.0 - lam_init)).astype(o_ref.dtype)


def _attention(z, cos_t, sin_t, lam_qk, attn_norm_g, batch, seq, lam_init):
    m = z.shape[0]
    tq = tk = 512
    nq = seq // tq
    kcol0 = COL_K // ATTN_V_DIM
    vcol0 = COL_V // ATTN_V_DIM
    return pl.pallas_call(
        functools.partial(_attn_kernel, tq=tq, tk=tk, rc=128, lam_init=lam_init),
        grid=(batch, ATTN_HEADS, nq),
        in_specs=[pl.BlockSpec((tq, ATTN_V_DIM), lambda b, h, i: (b * nq + i, h)),
                  pl.BlockSpec((seq, ATTN_V_DIM), lambda b, h, i: (b, kcol0 + h)),
                  pl.BlockSpec((seq, ATTN_V_DIM), lambda b, h, i: (b, vcol0 + h)),
                  pl.BlockSpec((seq, HEAD_DIM), lambda b, h, i: (0, 0)),
                  pl.BlockSpec((seq, HEAD_DIM), lambda b, h, i: (0, 0)),
                  pl.BlockSpec((4, HEAD_DIM), lambda b, h, i: (0, 0)),
                  pl.BlockSpec((1, ATTN_V_DIM), lambda b, h, i: (0, 0))],
        out_specs=pl.BlockSpec((tq, ATTN_V_DIM), lambda b, h, i: (b * nq + i, h)),
        out_shape=jax.ShapeDtypeStruct((m, ATTN_WIDTH), BF16),
        scratch_shapes=[pltpu.VMEM((seq // tk, ATTN_V_DIM, tk), BF16),
                        pltpu.VMEM((2 * tq, ATTN_V_DIM), BF16),
                        pltpu.VMEM((2 * tq, tk), F32),
                        pltpu.VMEM((2 * tq, tk), F32),
                        pltpu.VMEM((2 * tq, V7X_LANES), F32),
                        pltpu.VMEM((2 * tq, V7X_LANES), F32),
                        pltpu.VMEM((2 * tq, ATTN_V_DIM), F32)],
        compiler_params=_params("arbitrary", "arbitrary", "arbitrary"),
        name="diff_attn",
    )(z, z, z, cos_t, sin_t, lam_qk, attn_norm_g.reshape(1, ATTN_V_DIM))


def _gelu(x):
    return 0.5 * x * (1.0 + lax.erf(x * (2.0 ** -0.5)))


def _sgu_kernel(zu_ref, zv_ref, g_ref, b_ref, w_ref, bt_ref, o_ref, *, tm):
    sv = _gelu(zv_ref[...].astype(F32))
    mu = jnp.mean(sv, axis=-1, keepdims=True)
    var = jnp.mean(jnp.square(sv - mu), axis=-1, keepdims=True)
    sv = ((sv - mu) * lax.rsqrt(var + LN_EPS) * g_ref[...] + b_ref[...]).astype(BF16)
    row = lax.broadcasted_iota(jnp.int32, (SGU_CHUNK, SGU_CHUNK), 0)
    col = lax.broadcasted_iota(jnp.int32, (SGU_CHUNK, SGU_CHUNK), 1)
    bt = bt_ref[...]
    for g in range(SGU_GROUPS):
        cols = slice(g * V7X_LANES, (g + 1) * V7X_LANES)
        ws = jnp.where(col <= row, w_ref[g], 0.0).astype(BF16)
        bias = bt[:, g:g + 1]
        for ch in range(tm // SGU_CHUNK):
            rows = slice(ch * SGU_CHUNK, (ch + 1) * SGU_CHUNK)
            sp = _dot(ws, sv[rows, cols]) + bias
            u = _gelu(zu_ref[rows, cols].astype(F32))
            o_ref[rows, cols] = (u * sp).astype(o_ref.dtype)


def _sgu(z, ln_g, ln_b, sgu_w, sgu_b):
    m = z.shape[0]
    tm = 512
    ucol = COL_SGU // SGU_WIDTH
    return pl.pallas_call(
        functools.partial(_sgu_kernel, tm=tm),
        grid=(m // tm,),
        in_specs=[pl.BlockSpec((tm, SGU_WIDTH), lambda i: (i, ucol)),
                  pl.BlockSpec((tm, SGU_WIDTH), lambda i: (i, ucol + 1)),
                  pl.BlockSpec((1, SGU_WIDTH), lambda i: (0, 0)),
                  pl.BlockSpec((1, SGU_WIDTH), lambda i: (0, 0)),
                  pl.BlockSpec((SGU_GROUPS, SGU_CHUNK, SGU_CHUNK), lambda i: (0, 0, 0)),
                  pl.BlockSpec((SGU_CHUNK, SGU_GROUPS), lambda i: (0, 0))],
        out_specs=pl.BlockSpec((tm, SGU_WIDTH), lambda i: (i, 0)),
        out_shape=jax.ShapeDtypeStruct((m, SGU_WIDTH), BF16),
        compiler_params=_params("parallel"),
        name="sgu",
    )(z, z, ln_g.reshape(1, -1), ln_b.reshape(1, -1), sgu_w, sgu_b.T)


def _conv_kernel(za_ref, zb_ref, w_ref, b_ref, g_ref, beta_ref, o_ref, ybuf, yconv, *, tm, per_batch):
    i = pl.program_id(0)

    @pl.when(i % per_batch == 0)
    def _():
        ybuf[0:HALO, :] = jnp.zeros((HALO, CONV_WIDTH), F32)

    ybuf[HALO:HALO + tm, :] = za_ref[...].astype(F32) * _sigmoid(zb_ref[...].astype(F32))

    rc = 128
    for c in range(CONV_WIDTH // V7X_LANES):
        cols = slice(c * V7X_LANES, (c + 1) * V7X_LANES)
        w = w_ref[:, cols]
        for r in range(tm // rc):
            base = HALO + r * rc - V7X_SUBLANES
            acc = jnp.broadcast_to(b_ref[:, cols], (rc, V7X_LANES))
            for b in range(V7X_SUBLANES):
                u = None
                for k in range(b, CONV_KERNEL, V7X_SUBLANES):
                    tap = CONV_KERNEL - 1 - k
                    term = w[tap:tap + 1, :] * ybuf[base - (k - b):base - (k - b) + rc + V7X_SUBLANES, cols]
                    u = term if u is None else u + term
                acc = acc + u[V7X_SUBLANES - b:V7X_SUBLANES - b + rc, :]
            yconv[r * rc:(r + 1) * rc, cols] = acc

    ybuf[0:HALO, :] = ybuf[tm:tm + HALO, :]

    y = yconv[...]
    mu = jnp.mean(y, axis=-1, keepdims=True)
    var = jnp.mean(jnp.square(y - mu), axis=-1, keepdims=True)
    yn = (y - mu) * lax.rsqrt(var + LN_EPS) * g_ref[...] + beta_ref[...]
    o_ref[...] = (yn * _sigmoid(yn)).astype(o_ref.dtype)


def _conv_branch(z, w, b, ln_g, ln_b, seq):
    m = z.shape[0]
    tm = 512
    acol = COL_CONV // CONV_WIDTH
    return pl.pallas_call(
        functools.partial(_conv_kernel, tm=tm, per_batch=seq // tm),
        grid=(m // tm,),
        in_specs=[pl.BlockSpec((tm, CONV_WIDTH), lambda i: (i, acol)),
                  pl.BlockSpec((tm, CONV_WIDTH), lambda i: (i, acol + 1)),
                  pl.BlockSpec((CONV_KERNEL, CONV_WIDTH), lambda i: (0, 0)),
                  pl.BlockSpec((1, CONV_WIDTH), lambda i: (0, 0)),
                  pl.BlockSpec((1, CONV_WIDTH), lambda i: (0, 0)),
                  pl.BlockSpec((1, CONV_WIDTH), lambda i: (0, 0))],
        out_specs=pl.BlockSpec((tm, CONV_WIDTH), lambda i: (i, 0)),
        out_shape=jax.ShapeDtypeStruct((m, CONV_WIDTH), BF16),
        scratch_shapes=[pltpu.VMEM((HALO + tm, CONV_WIDTH), F32),
                        pltpu.VMEM((tm, CONV_WIDTH), F32)],
        compiler_params=_params("arbitrary"),
        name="conv_branch",
    )(z, z, w, b.reshape(1, -1), ln_g.reshape(1, -1), ln_b.reshape(1, -1))


def _merge_kernel(oa_ref, os_ref, oc_ref, wa_ref, ws_ref, wc_ref,
                  ga_ref, gs_ref, gc_ref, bg_ref, o_ref):
    def gate(z_ref, idx):
        return _sigmoid(z_ref[...].astype(F32) + bg_ref[idx:idx + 1, :])

    acc = gate(ga_ref, 0) * _dot(oa_ref[...], wa_ref[...])
    acc = acc + gate(gs_ref, 1) * _dot(os_ref[...], ws_ref[...])
    acc = acc + gate(gc_ref, 2) * _dot(oc_ref[...], wc_ref[...])
    o_ref[...] = acc.astype(o_ref.dtype)


def _merge(o_attn, o_sgu, o_conv, wpa, wps, wpc, z, b_gates3):
    m = z.shape[0]
    d = D_MODEL
    tm, tn = 512, 1024
    gcol0 = COL_GATE // tn
    nb = d // tn
    return pl.pallas_call(
        _merge_kernel,
        grid=(d // tn, m // tm),
        in_specs=[pl.BlockSpec((tm, ATTN_WIDTH), lambda j, i: (i, 0)),
                  pl.BlockSpec((tm, SGU_WIDTH), lambda j, i: (i, 0)),
                  pl.BlockSpec((tm, CONV_WIDTH), lambda j, i: (i, 0)),
                  pl.BlockSpec((ATTN_WIDTH, tn), lambda j, i: (0, j)),
                  pl.BlockSpec((SGU_WIDTH, tn), lambda j, i: (0, j)),
                  pl.BlockSpec((CONV_WIDTH, tn), lambda j, i: (0, j)),
                  pl.BlockSpec((tm, tn), lambda j, i: (i, gcol0 + j)),
                  pl.BlockSpec((tm, tn), lambda j, i: (i, gcol0 + nb + j)),
                  pl.BlockSpec((tm, tn), lambda j, i: (i, gcol0 + 2 * nb + j)),
                  pl.BlockSpec((3, tn), lambda j, i: (0, j))],
        out_specs=pl.BlockSpec((tm, tn), lambda j, i: (i, j)),
        out_shape=jax.ShapeDtypeStruct((m, d), BF16),
        compiler_params=_params("parallel", "parallel"),
        name="merge",
    )(o_attn, o_sgu, o_conv, wpa, wps, wpc, z, z, z, b_gates3)


def _proj_res_kernel(a_ref, w_ref, x_ref, mod_ref, o_ref, *, gate_row):
    gate = mod_ref[gate_row:gate_row + 1, :]
    o_ref[...] = x_ref[...] + gate * _dot(a_ref[...], w_ref[...])


def _proj_residual(a, w, x, mod, seq, *, gate_row):
    m, k = a.shape
    d = w.shape[1]
    tm, tn = 1024, 512
    per_batch = seq // tm
    return pl.pallas_call(
        functools.partial(_proj_res_kernel, gate_row=gate_row),
        grid=(m // tm, d // tn),
        in_specs=[pl.BlockSpec((tm, k), lambda i, j: (i, 0)),
                  pl.BlockSpec((k, tn), lambda i, j: (0, j)),
                  pl.BlockSpec((tm, tn), lambda i, j: (i, j)),
                  pl.BlockSpec((None, 6, tn), lambda i, j: (i // per_batch, 0, j))],
        out_specs=pl.BlockSpec((tm, tn), lambda i, j: (i, j)),
        out_shape=jax.ShapeDtypeStruct((m, d), F32),
        compiler_params=_params("parallel", "parallel"),
        name="proj_residual",
    )(a, w, x, mod)


def _ffn_up_kernel(h_ref, a0_ref, a1_ref, b0_ref, b1_ref, cw_ref, cb_ref, wd_ref, o_ref, owd_ref, w0, w1, fbuf,
                   *, tm, tn, ck, n_tiles, per_batch, n_wd_real):
    jj = pl.program_id(0)
    i = pl.program_id(1)
    rows = pl.ds(pl.multiple_of(i * ck, ck), ck)
    half = tn // 2
    past_end = jj >= n_tiles - 1
    wd_is_pad = jj * pl.num_programs(1) + i >= n_wd_real

    @pl.when(i % per_batch == 0)
    def _():
        fbuf[0:V7X_SUBLANES, :] = jnp.zeros((V7X_SUBLANES, tn), F32)

    def fill(dst):
        dst[rows, 0:half] = a0_ref[...].astype(BF16)
        dst[rows, half:tn] = jnp.where(past_end, 0.0, a1_ref[...]).astype(BF16)
        dst[rows, tn:tn + half] = b0_ref[...].astype(BF16)
        dst[rows, tn + half:2 * tn] = jnp.where(past_end, 0.0, b1_ref[...]).astype(BF16)
        owd_ref[...] = jnp.where(wd_is_pad, 0.0, wd_ref[...]).astype(BF16)

    def compute(src):
        fab = _dot(h_ref[...], src[...])
        fbuf[V7X_SUBLANES:V7X_SUBLANES + tm, :] = fab[:, 0:tn]
        cw = cw_ref[...]
        y = cb_ref[...] + cw[2:3, :] * fbuf[V7X_SUBLANES:V7X_SUBLANES + tm, :]
        y = y + cw[1:2, :] * fbuf[V7X_SUBLANES - 1:V7X_SUBLANES - 1 + tm, :]
        y = y + cw[0:1, :] * fbuf[V7X_SUBLANES - 2:V7X_SUBLANES - 2 + tm, :]
        fbuf[0:V7X_SUBLANES, :] = fbuf[tm:tm + V7X_SUBLANES, :]
        o_ref[...] = (y * _sigmoid(y) * fab[:, tn:2 * tn]).astype(o_ref.dtype)

    @pl.when(jj == 0)
    def _():
        fill(w0)

    @pl.when(jnp.logical_and(jj > 0, jj % 2 == 0))
    def _():
        fill(w0)
        compute(w1)

    @pl.when(jj % 2 == 1)
    def _():
        fill(w1)
        compute(w0)


def _ffn_up(h, ffn_up, l, cw, cb, ffn_down, seq):
    m, d = h.shape
    tm, tn = 1024, 512
    half = tn // 2
    n_row_tiles, n_tiles = m // tm, FFN_PAD // tn
    ck = d // n_row_tiles
    n_half_blocks = FFN_DIM // half
    last_block = 2 * n_half_blocks - 1
    d_out = ffn_down.shape[2]
    br = _side_block_rows((FFN_PAD,), (n_tiles + 1) * n_row_tiles)
    assert FFN_DIM % br == 0
    n_wd, n_wd_real = FFN_PAD // br, FFN_DIM // br

    def step(jj, i):
        return jj * n_row_tiles + i

    def wspec(offset):
        def index(jj, i):
            t = jnp.minimum(jj, n_tiles - 1)
            return (l, i, jnp.minimum(offset + 2 * t, last_block))
        return pl.BlockSpec((None, ck, half), index)

    def row_tile(jj, i):
        return jnp.where(jj > 0, i, 0)

    def col_tile(jj, i):
        return jnp.maximum(jj - 1, 0)

    return pl.pallas_call(
        functools.partial(_ffn_up_kernel, tm=tm, tn=tn, ck=ck, n_tiles=n_tiles, per_batch=seq // tm,
                          n_wd_real=n_wd_real),
        grid=(n_tiles + 1, n_row_tiles),
        in_specs=[pl.BlockSpec((tm, d), lambda jj, i: (row_tile(jj, i), 0)),
                  wspec(0), wspec(1), wspec(n_half_blocks), wspec(n_half_blocks + 1),
                  pl.BlockSpec((FFN_CONV_KERNEL, tn), lambda jj, i: (0, col_tile(jj, i))),
                  pl.BlockSpec((1, tn), lambda jj, i: (0, col_tile(jj, i))),
                  pl.BlockSpec((None, br, d_out), lambda jj, i: (l, jnp.minimum(step(jj, i), n_wd_real - 1), 0))],
        out_specs=[pl.BlockSpec((tm, tn), lambda jj, i: (row_tile(jj, i), col_tile(jj, i))),
                   pl.BlockSpec((br, d_out), lambda jj, i: (jnp.minimum(step(jj, i), n_wd - 1), 0))],
        out_shape=[jax.ShapeDtypeStruct((m, FFN_PAD), BF16),
                   jax.ShapeDtypeStruct((FFN_PAD, d_out), BF16)],
        scratch_shapes=[pltpu.VMEM((d, 2 * tn), BF16), pltpu.VMEM((d, 2 * tn), BF16),
                        pltpu.VMEM((V7X_SUBLANES + tm, tn), F32)],
        compiler_params=_params("arbitrary", "arbitrary"),
        name="ffn_up",
    )(h, ffn_up, ffn_up, ffn_up, ffn_up, cw, cb, ffn_down)


def _ffn_down_kernel(a_ref, w_ref, x_ref, mod_ref, o_ref, *, gate_row, nk):
    k = pl.program_id(2)

    @pl.when(k == 0)
    def _():
        o_ref[...] = _dot(a_ref[...], w_ref[...])

    @pl.when(jnp.logical_and(k > 0, k < nk - 1))
    def _():
        o_ref[...] += _dot(a_ref[...], w_ref[...])

    @pl.when(k == nk - 1)
    def _():
        gate = mod_ref[gate_row:gate_row + 1, :]
        o_ref[...] = x_ref[...] + gate * (o_ref[...] + _dot(a_ref[...], w_ref[...]))


def _ffn_down(a, w, x, mod, seq, *, gate_row):
    m, kdim = a.shape
    d = w.shape[1]
    tm, tn, tk = 1024, 1024, 2816
    nk = kdim // tk
    per_batch = seq // tm
    return pl.pallas_call(
        functools.partial(_ffn_down_kernel, gate_row=gate_row, nk=nk),
        grid=(m // tm, d // tn, nk),
        in_specs=[pl.BlockSpec((tm, tk), lambda i, j, k: (i, k)),
                  pl.BlockSpec((tk, tn), lambda i, j, k: (k, j)),
                  pl.BlockSpec((tm, tn), lambda i, j, k: (i, j)),
                  pl.BlockSpec((None, 6, tn), lambda i, j, k: (i // per_batch, 0, j))],
        out_specs=pl.BlockSpec((tm, tn), lambda i, j, k: (i, j)),
        out_shape=jax.ShapeDtypeStruct((m, d), F32),
        compiler_params=_params("parallel", "parallel", "arbitrary"),
        name="ffn_down",
    )(a, w, x, mod)


def kernel(x, c, ada_w, ada_b, norm1_g, w_in, b_gates, lam_qk, attn_norm_g, sgu_ln_g, sgu_ln_b, sgu_w, sgu_b, conv_dw_w, conv_dw_b, conv_ln_g, conv_ln_b, w_proj_attn, w_proj_sgu, w_proj_conv, w_out, norm2_g, ffn_up, ffn_dw_w, ffn_dw_b, ffn_down, final_g):
    batch, seq, d = x.shape
    depth = ada_w.shape[0]
    assert (d, seq % 1024) == (D_MODEL, 0)
    m = batch * seq
    xf = x.reshape(m, d)

    c_pad = jnp.zeros((V7X_SUBLANES, d), F32).at[:batch].set(c)
    mod_all = _ada_mod(c_pad, ada_w, ada_b)

    half = HEAD_DIM // 2
    inv = ROPE_THETA ** (-jnp.arange(half, dtype=F32) / half)
    ang = jnp.arange(seq, dtype=F32)[:, None] * inv[None, :]
    cos_t = jnp.concatenate([jnp.cos(ang), jnp.cos(ang)], axis=-1)
    sin_t = jnp.concatenate([-jnp.sin(ang), jnp.sin(ang)], axis=-1)

    pad = FFN_PAD - FFN_DIM
    for l in range(depth):
        mod = mod_all[l, :batch].reshape(batch, 6, d)
        lam_init = 0.8 - 0.6 * math.exp(-0.3 * l)

        h = _norm_mod(xf, norm1_g[l], mod, seq, shift_row=0, scale_row=1)
        z, wpa, wo = _in_proj(h, w_in, l, w_proj_attn, w_out)
        o_attn = _attention(z, cos_t, sin_t, lam_qk[l], attn_norm_g[l], batch, seq, lam_init)
        o_sgu = _sgu(z, sgu_ln_g[l], sgu_ln_b[l], sgu_w[l], sgu_b[l])
        o_conv = _conv_branch(z, conv_dw_w[l], conv_dw_b[l], conv_ln_g[l], conv_ln_b[l], seq)
        merged = _merge(o_attn, o_sgu, o_conv, wpa,
                        _cast_weight(w_proj_sgu, l, tr=256, tn=d), _cast_weight(w_proj_conv, l, tr=256, tn=d),
                        z, b_gates[l].reshape(3, d))
        xf = _proj_residual(merged, wo, xf, mod, seq, gate_row=2)

        h2 = _norm_mod(xf, norm2_g[l], mod, seq, shift_row=3, scale_row=4)
        cw = jnp.pad(ffn_dw_w[l], ((0, 0), (0, pad)))
        cb = jnp.pad(ffn_dw_b[l], (0, pad)).reshape(1, FFN_PAD)
        act, wd = _ffn_up(h2, ffn_up, l, cw, cb, ffn_down, seq)
        xf = _ffn_down(act, wd, xf, mod, seq, gate_row=5)

    return _final_norm(xf, final_g).reshape(batch, seq, d)
```
